```python
import math
import jax, jax.numpy as jnp
from jax import lax
import numpy as np

D_MODEL = 1024
BATCH = 2
SEQ = 8192
DEPTH = 4
DEC_BATCH = 32
DEC_SEQ = 1
PAST_LEN = 8192
PAGE_SIZE = 128

H_A = 4
DK_A = 64
DV_A = 2 * DK_A
H_B = 4
D_B = 64
H_IDX = 8
D_IDX = 64
TOPK_MAX = 256
H_C = 4
D_C = 64

N_BRANCH = 3
ROPE_THETA = 500000.0
ROT_FRAC = 4
Q_BLOCK = 128
D_FF = ((-(-8 * D_MODEL // 3) + 255) // 256) * 256
EPS = 1e-6
NEG = -1e30
N_CACHE = 8

IN_WIDTHS = (H_A * 2 * DK_A, H_A * 2 * DK_A, H_A * DV_A,
             H_B * D_B, H_B * D_B, H_B * D_B,
             H_IDX * D_IDX, D_IDX, H_IDX,
             H_C * D_C, H_C * D_C, H_C * D_C, H_C,
             N_BRANCH * D_MODEL)
N_IN = sum(IN_WIDTHS)

kernel_name = "hybrid_diff_dsa_fox_gated_decoder_step"


def rms_norm(x, g):
    xf = x.astype(jnp.float32)
    y = xf * lax.rsqrt(jnp.mean(xf * xf, axis=-1, keepdims=True) + EPS)
    return (y * g.astype(jnp.float32)).astype(x.dtype)


def rope_partial(x, pos):
    d = x.shape[-1]
    r = d // ROT_FRAC
    half = r // 2
    t = x.shape[1]
    inv = jnp.power(ROPE_THETA, -jnp.arange(half, dtype=jnp.float32) * 2.0 / r)
    ang = pos.astype(jnp.float32)[:, None] * inv[None, :]
    bshape = (1, t) + (1,) * (x.ndim - 3) + (half,)
    cos = jnp.cos(ang).reshape(bshape)
    sin = jnp.sin(ang).reshape(bshape)
    xf = x.astype(jnp.float32)
    x1 = xf[..., :half]
    x2 = xf[..., half:r]
    out = jnp.concatenate([x1 * cos - x2 * sin, x1 * sin + x2 * cos, xf[..., r:]], axis=-1)
    return out.astype(x.dtype)


def sweep_queries(fn, qs, q_pos, block):
    t = q_pos.shape[0]
    nb = t // block

    def split(a):
        return jnp.moveaxis(a.reshape((a.shape[0], nb, block) + a.shape[2:]), 1, 0)

    xs = tuple(split(a) for a in qs) + (q_pos.reshape(nb, block),)
    out = lax.map(lambda args: fn(*args), xs)
    out = jnp.moveaxis(out, 0, 1)
    return out.reshape((out.shape[0], t) + out.shape[3:])


def diff_attn_block(q1, q2, k1, k2, v, lam, q_pos, k_pos):
    mask = k_pos[None, :] <= q_pos[:, None]
    scale = DK_A ** -0.5

    def probs(q, k):
        s = jnp.einsum('bqhd,bkhd->bhqk', q, k).astype(jnp.float32) * scale
        return jax.nn.softmax(jnp.where(mask, s, NEG), axis=-1)

    p = probs(q1, k1) - lam * probs(q2, k2)
    return jnp.einsum('bhqk,bkhe->bqhe', p.astype(v.dtype), v)


def dsa_block(q, qi, w, kidx, k, v, q_pos, k_pos, top_k):
    isc = jax.nn.relu(jnp.einsum('bqhd,bkd->bqkh', qi, kidx).astype(jnp.float32) * (D_IDX ** -0.5))
    isc = jnp.einsum('bqkh,bqh->bqk', isc, w.astype(jnp.float32))
    causal = k_pos[None, :] <= q_pos[:, None]
    isc = jnp.where(causal[None], isc, NEG)
    _, idx = lax.top_k(isc, top_k)
    ksel = jax.vmap(lambda kb, ib: kb[ib])(k, idx)
    vsel = jax.vmap(lambda vb, ib: vb[ib])(v, idx)
    valid = jnp.take(k_pos, idx) <= q_pos[None, :, None]
    s = jnp.einsum('bqhd,bqkhd->bhqk', q, ksel).astype(jnp.float32) * (D_B ** -0.5)
    p = jax.nn.softmax(jnp.where(valid[:, None], s, NEG), axis=-1)
    return jnp.einsum('bhqk,bqkhd->bqhd', p.astype(v.dtype), vsel)


def fox_block(q, k, v, cq, ck, q_pos, k_pos):
    mask = k_pos[None, :] <= q_pos[:, None]
    s = jnp.einsum('bqhd,bkhd->bhqk', q, k).astype(jnp.float32) * (D_C ** -0.5)
    decay = jnp.transpose(cq, (0, 2, 1))[..., :, None] - jnp.transpose(ck, (0, 2, 1))[..., None, :]
    p = jax.nn.softmax(jnp.where(mask, s + decay, NEG), axis=-1)
    return jnp.einsum('bhqk,bkhd->bqhd', p.astype(v.dtype), v)


def gather_pages(pool, page_table):
    g = pool[page_table]
    return g.reshape((g.shape[0], g.shape[1] * g.shape[2]) + g.shape[3:])


def trunk_layer(x, pos, past, lp, lam_init):
    (norm_attn, w_in, b_forget, q_norm_a, k_norm_a, lambda_q1, lambda_k1, lambda_q2, lambda_k2,
     subln_a, q_norm_b, k_norm_b, k_norm_idx, q_norm_c, k_norm_c, w_out_a, w_out_b, w_out_c,
     w_o, norm_ffn, w_gate, w_up, w_down) = lp
    f32 = jnp.float32
    bsz, t, _ = x.shape
    xn = rms_norm(x, norm_attn)
    proj = xn @ w_in
    split_points = np.cumsum(IN_WIDTHS)[:-1].tolist()
    (aq, ak, av, bq, bk, bv, iq, ik, iw, cq, ck, cv, cf, gl) = jnp.split(proj, split_points, axis=-1)
    aq = rope_partial(rms_norm(aq.reshape(bsz, t, H_A, 2, DK_A), q_norm_a), pos)
    ak = rope_partial(rms_norm(ak.reshape(bsz, t, H_A, 2, DK_A), k_norm_a), pos).reshape(bsz, t, H_A, 2 * DK_A)
    av = av.reshape(bsz, t, H_A, DV_A)
    bq = rope_partial(rms_norm(bq.reshape(bsz, t, H_B, D_B), q_norm_b), pos)
    bk = rope_partial(rms_norm(bk.reshape(bsz, t, H_B, D_B), k_norm_b), pos)
    bv = bv.reshape(bsz, t, H_B, D_B)
    iq = rope_partial(iq.reshape(bsz, t, H_IDX, D_IDX), pos)
    ik = rope_partial(rms_norm(ik, k_norm_idx), pos)
    iw = iw * (H_IDX ** -0.5)
    cq = rms_norm(cq.reshape(bsz, t, H_C, D_C), q_norm_c)
    ck = rms_norm(ck.reshape(bsz, t, H_C, D_C), k_norm_c)
    cv = cv.reshape(bsz, t, H_C, D_C)
    logf = jax.nn.log_sigmoid(cf.astype(f32) + b_forget.astype(f32))
    rows = (ak, av, bk, bv, ik, ck, cv, logf.astype(x.dtype))

    if past is None:
        a_k, a_v, b_k, b_v, i_k, c_k, c_v = ak, av, bk, bv, ik, ck, cv
        k_pos = pos
        c_q = jnp.cumsum(logf, axis=1)
        c_k_cum = c_q
    else:
        (pa_k, pa_v, pb_k, pb_v, pi_k, pc_k, pc_v, pc_logf) = past

        def cat(p, r):
            return jnp.concatenate([p, r.astype(p.dtype)], axis=1)

        a_k, a_v, b_k, b_v, i_k, c_k, c_v = (cat(pa_k, ak), cat(pa_v, av), cat(pb_k, bk), cat(pb_v, bv),
                                             cat(pi_k, ik), cat(pc_k, ck), cat(pc_v, cv))
        n_past = pa_k.shape[1]
        k_pos = jnp.arange(n_past + t, dtype=jnp.int32)
        c_past = jnp.cumsum(pc_logf.astype(f32), axis=1)
        c_q = c_past[:, -1:] + jnp.cumsum(logf, axis=1)
        c_k_cum = jnp.concatenate([c_past, c_q], axis=1)

    n_keys = k_pos.shape[0]
    top_k = min(TOPK_MAX, n_keys // 4)
    block = Q_BLOCK if t % Q_BLOCK == 0 else t

    lam = (jnp.exp(jnp.sum(lambda_q1.astype(f32) * lambda_k1.astype(f32)))
           - jnp.exp(jnp.sum(lambda_q2.astype(f32) * lambda_k2.astype(f32))) + lam_init)
    k1 = a_k[..., :DK_A]
    k2 = a_k[..., DK_A:]
    o_a = sweep_queries(lambda q1, q2, qp: diff_attn_block(q1, q2, k1, k2, a_v, lam, qp, k_pos),
                        (aq[..., 0, :], aq[..., 1, :]), pos, block)
    o_a = rms_norm(o_a, subln_a) * (1.0 - lam_init)
    o_b = sweep_queries(lambda q, qi, w, qp: dsa_block(q, qi, w, i_k, b_k, b_v, qp, k_pos, top_k),
                        (bq, iq, iw), pos, block)
    o_c = sweep_queries(lambda q, cqb, qp: fox_block(q, c_k, c_v, cqb, c_k_cum, qp, k_pos),
                        (cq, c_q), pos, block)

    g = jax.nn.sigmoid(gl.astype(f32)).astype(x.dtype).reshape(bsz, t, N_BRANCH, D_MODEL)
    merged = (g[:, :, 0] * (o_a.reshape(bsz, t, -1) @ w_out_a)
              + g[:, :, 1] * (o_b.reshape(bsz, t, -1) @ w_out_b)
              + g[:, :, 2] * (o_c.reshape(bsz, t, -1) @ w_out_c))
    h = x + merged @ w_o
    hn = rms_norm(h, norm_ffn)
    y = h + (jax.nn.silu(hn @ w_gate) * (hn @ w_up)) @ w_down
    return y, rows


def setup_inputs(seed: int = 0) -> dict:
    key = jax.random.key(seed)
    ks = iter(jax.random.split(key, 48))
    f32 = jnp.float32
    n_pages = PAST_LEN // PAGE_SIZE
    n_used = DEC_BATCH * n_pages
    n_pool = n_used + (n_used + 3) // 4

    def nrm(shape, scale=1.0):
        return jax.random.normal(next(ks), shape, f32) * scale

    def gain(n):
        return 1.0 + 0.02 * nrm((DEPTH, n))

    pool = (DEPTH, n_pool, PAGE_SIZE)
    page_table = jax.random.permutation(next(ks), n_pool)[:n_used].reshape(DEC_BATCH, n_pages).astype(jnp.int32)
    return {
        "x_prompt": nrm((BATCH, SEQ, D_MODEL)),
        "x_sample": nrm((DEC_BATCH, DEC_SEQ, D_MODEL)),
        "cache_a_k": nrm(pool + (H_A, 2 * DK_A)),
        "cache_a_v": nrm(pool + (H_A, DV_A)),
        "cache_b_k": nrm(pool + (H_B, D_B)),
        "cache_b_v": nrm(pool + (H_B, D_B)),
        "cache_b_kidx": nrm(pool + (D_IDX,)),
        "cache_c_k": nrm(pool + (H_C, D_C)),
        "cache_c_v": nrm(pool + (H_C, D_C)),
        "cache_c_logf": jax.nn.log_sigmoid(2.0 + nrm(pool + (H_C,))),
        "page_table": page_table,
        "norm_attn": gain(D_MODEL),
        "w_in": nrm((DEPTH, D_MODEL, N_IN), D_MODEL ** -0.5),
        "b_forget": 2.0 + 0.5 * nrm((DEPTH, H_C)),
        "q_norm_a": gain(DK_A),
        "k_norm_a": gain(DK_A),
        "lambda_q1": nrm((DEPTH, DK_A), 0.1),
        "lambda_k1": nrm((DEPTH, DK_A), 0.1),
        "lambda_q2": nrm((DEPTH, DK_A), 0.1),
        "lambda_k2": nrm((DEPTH, DK_A), 0.1),
        "subln_a": gain(DV_A),
        "q_norm_b": gain(D_B),
        "k_norm_b": gain(D_B),
        "k_norm_idx": gain(D_IDX),
        "q_norm_c": gain(D_C),
        "k_norm_c": gain(D_C),
        "w_out_a": nrm((DEPTH, H_A * DV_A, D_MODEL), (H_A * DV_A) ** -0.5),
        "w_out_b": nrm((DEPTH, H_B * D_B, D_MODEL), (H_B * D_B) ** -0.5),
        "w_out_c": nrm((DEPTH, H_C * D_C, D_MODEL), (H_C * D_C) ** -0.5),
        "w_o": nrm((DEPTH, D_MODEL, D_MODEL), D_MODEL ** -0.5),
        "norm_ffn": gain(D_MODEL),
        "w_gate": nrm((DEPTH, D_MODEL, D_FF), D_MODEL ** -0.5),
        "w_up": nrm((DEPTH, D_MODEL, D_FF), D_MODEL ** -0.5),
        "w_down": nrm((DEPTH, D_FF, D_MODEL), D_FF ** -0.5),
    }


def reference(x_prompt, x_sample, cache_a_k, cache_a_v, cache_b_k, cache_b_v, cache_b_kidx,
              cache_c_k, cache_c_v, cache_c_logf, page_table, norm_attn, w_in, b_forget,
              q_norm_a, k_norm_a, lambda_q1, lambda_k1, lambda_q2, lambda_k2, subln_a,
              q_norm_b, k_norm_b, k_norm_idx, q_norm_c, k_norm_c, w_out_a, w_out_b, w_out_c,
              w_o, norm_ffn, w_gate, w_up, w_down):
    pos_p = jnp.arange(x_prompt.shape[1], dtype=jnp.int32)
    pos_s = PAST_LEN + jnp.arange(x_sample.shape[1], dtype=jnp.int32)
    caches = (cache_a_k, cache_a_v, cache_b_k, cache_b_v, cache_b_kidx, cache_c_k, cache_c_v, cache_c_logf)
    x_p = x_prompt
    x_s = x_sample
    rows_p = []
    rows_s = []
    for l in range(DEPTH):
        lp = (norm_attn[l], w_in[l], b_forget[l], q_norm_a[l], k_norm_a[l], lambda_q1[l], lambda_k1[l],
              lambda_q2[l], lambda_k2[l], subln_a[l], q_norm_b[l], k_norm_b[l], k_norm_idx[l],
              q_norm_c[l], k_norm_c[l], w_out_a[l], w_out_b[l], w_out_c[l], w_o[l], norm_ffn[l],
              w_gate[l], w_up[l], w_down[l])
        lam_init = 0.8 - 0.6 * math.exp(-0.3 * l)
        x_p, r_p = trunk_layer(x_p, pos_p, None, lp, lam_init)
        past = tuple(gather_pages(c[l], page_table) for c in caches)
        x_s, r_s = trunk_layer(x_s, pos_s, past, lp, lam_init)
        rows_p.append(r_p)
        rows_s.append(r_s)
    p_a_k, p_a_v, p_b_k, p_b_v, p_b_kidx, p_c_k, p_c_v, p_c_logf = [
        jnp.stack([r[i] for r in rows_p], axis=0) for i in range(N_CACHE)]
    s_a_k, s_a_v, s_b_k, s_b_v, s_b_kidx, s_c_k, s_c_v, s_c_logf = [
        jnp.stack([r[i] for r in rows_s], axis=0) for i in range(N_CACHE)]
    return (x_p, x_s, p_a_k, p_a_v, p_b_k, p_b_v, p_b_kidx, p_c_k, p_c_v, p_c_logf,
            s_a_k, s_a_v, s_b_k, s_b_v, s_b_kidx, s_c_k, s_c_v, s_c_logf)
```

```python
import functools
import math

import numpy as np
import jax
import jax.numpy as jnp
from jax import lax
from jax.experimental import pallas as pl
from jax.experimental.pallas import tpu as pltpu

F32 = jnp.float32
BF16 = jnp.bfloat16

D_MODEL = 1024
H_A, DK_A, DV_A = 4, 64, 128
H_B, D_B = 4, 64
H_IDX, D_IDX = 8, 64
TOPK_MAX = 256
H_C, D_C = 4, 64
N_BRANCH = 3
ROPE_THETA = 500000.0
ROT_FRAC = 4
EPS = 1e-6
NEG = -1e30
IN_WIDTHS = (H_A * 2 * DK_A, H_A * 2 * DK_A, H_A * DV_A,
             H_B * D_B, H_B * D_B, H_B * D_B,
             H_IDX * D_IDX, D_IDX, H_IDX,
             H_C * D_C, H_C * D_C, H_C * D_C, H_C,
             N_BRANCH * D_MODEL)

LANES = 128
HEAD = 64
QK_SCALE = HEAD ** -0.5

OFF_AQ, OFF_AK, OFF_AV = 0, 512, 1024
OFF_BQ, OFF_BK, OFF_BV = 1536, 1792, 2048
OFF_IQ, OFF_MISC = 2304, 2816
OFF_CQ, OFF_CK, OFF_CV = 2944, 3200, 3456
W_MIX = 3712
MISC_IW, MISC_CF = 64, 72

TM_PROJ = 512
TQ_A = 512
TQ_C = 512
TQ_B = 256
TM_MERGE = 512
TM_FFN = 512
VMEM_LIMIT = 56 * 1024 * 1024


def _nt(a, b):
    return lax.dot_general(a, b, (((1,), (1,)), ((), ())), preferred_element_type=F32)


def _dot(a, b):
    return jnp.dot(a, b, preferred_element_type=F32)


def _const_spec(shape, index):
    n = len(index)
    return pl.BlockSpec(shape, lambda *_: index, pipeline_mode=pl.Buffered(1))


def _proj_kernel(x_ref, g_ref, w_ref, pv_ref, bd_ref, cos_ref, sa_ref, sb_ref, tri_ref,
                 ak_o, av_o, bk_o, bv_o, ik_o, ck_o, cv_o, lf_o,
                 aq1_o, aq2_o, akb_o, avb_o, bqz_o, bkb_o, bvb_o, iqz_o, ik2_o, iw_o,
                 cqz_o, ckb_o, cvb_o, cc_o, carry_ref, *, tiles_per_seq):
    i = pl.program_id(0)
    x = x_ref[...]
    ms = jnp.mean(x * x, axis=-1, keepdims=True)
    xn = (x * lax.rsqrt(ms + EPS) * g_ref[...]).astype(BF16)
    cos = cos_ref[...]
    sa = sa_ref[...]
    sb = sb_ref[...]
    lane = lax.broadcasted_iota(jnp.int32, (1, LANES), 1)
    lo = jnp.where(lane < HEAD, 1.0, 0.0).astype(F32)
    hi = 1.0 - lo

    def mm(off, w):
        return _dot(xn, w_ref[:, off:off + w])

    def tile(t, w):
        return t if w == LANES else jnp.concatenate([t] * (w // LANES), axis=1)

    def norm(p, row, w):
        x2 = p * p
        h = x2.astype(BF16)
        l = (x2 - h.astype(F32)).astype(BF16)
        bd = bd_ref[0:w, 0:w]
        m = _dot(h, bd) + _dot(l, bd)
        return p * lax.rsqrt(m + EPS) * pv_ref[row:row + 1, 0:w]

    def rope(y, w):
        return (y * tile(cos, w) + pltpu.roll(y, w - 8, 1) * tile(sa, w)
                + pltpu.roll(y, 8, 1) * tile(sb, w))

    def pad_heads(y, n_pairs):
        parts = []
        for j in range(n_pairs):
            col = y[:, j * LANES:(j + 1) * LANES]
            parts.append(col * lo)
            parts.append(col * hi)
        return jnp.concatenate(parts, axis=1)

    aq = rope(norm(mm(OFF_AQ, 512), 0, 512), 512) * QK_SCALE
    aq1_o[...] = (aq * tile(lo, 512)).astype(BF16)
    aq2_o[...] = (aq * tile(hi, 512)).astype(BF16)
    ak = rope(norm(mm(OFF_AK, 512), 1, 512), 512)
    ak_o[...] = ak
    akb_o[...] = ak.astype(BF16)
    av = mm(OFF_AV, 512)
    av_o[...] = av
    avb_o[...] = av.astype(BF16)
    bq = rope(norm(mm(OFF_BQ, 256), 2, 256), 256) * QK_SCALE
    bqz_o[...] = pad_heads(bq, 2).astype(BF16)
    bk = rope(norm(mm(OFF_BK, 256), 3, 256), 256)
    bk_o[...] = bk
    bkb_o[...] = bk.astype(BF16)
    bv = mm(OFF_BV, 256)
    bv_o[...] = bv
    bvb_o[...] = bv.astype(BF16)
    iq = rope(mm(OFF_IQ, 512), 512) * QK_SCALE
    iqz_o[...] = pad_heads(iq, 4).astype(BF16)
    misc = mm(OFF_MISC, LANES)
    ik = rope(norm(misc, 6, LANES), LANES)
    ik_o[...] = ik[:, 0:D_IDX]
    ik2_o[...] = jnp.where(lane < HEAD, ik, pltpu.roll(ik, HEAD, 1)).astype(BF16)
    iw_o[...] = misc[:, MISC_IW:MISC_IW + H_IDX] * (H_IDX ** -0.5)
    cq = norm(mm(OFF_CQ, 256), 4, 256) * QK_SCALE
    cqz_o[...] = pad_heads(cq, 2).astype(BF16)
    ck = norm(mm(OFF_CK, 256), 5, 256)
    ck_o[...] = ck
    ckb_o[...] = ck.astype(BF16)
    cv = mm(OFF_CV, 256)
    cv_o[...] = cv
    cvb_o[...] = cv.astype(BF16)
    z = misc + pv_ref[7:8, 0:LANES]
    lf = jnp.minimum(z, 0.0) - jnp.log1p(jnp.exp(-jnp.abs(z)))
    lf_o[...] = lf[:, MISC_CF:MISC_CF + H_C]

    @pl.when(i % tiles_per_seq == 0)
    def _():
        carry_ref[...] = jnp.zeros_like(carry_ref)

    cum = jnp.dot(tri_ref[...], lf, preferred_element_type=F32,
                  precision=lax.Precision.HIGHEST) + carry_ref[...]
    cc_o[...] = cum[:, MISC_CF:MISC_CF + H_C]
    tm = cum.shape[0]
    carry_ref[...] = cum[tm - 1:tm, :]


def _proj_call(x2d, layer, p, tabs, t_seq):
    n = x2d.shape[0]
    tm = min(TM_PROJ, n)
    assert n % tm == 0 and t_seq % tm == 0
    cos, sa, sb = tabs
    ntab = cos.shape[0] // tm
    tri = jnp.tril(jnp.ones((tm, tm), F32))
    row = lambda w: pl.BlockSpec((tm, w), lambda i: (i, 0))
    tab = pl.BlockSpec((tm, LANES), lambda i: (i % ntab, 0))
    widths_f32 = (512, 512, 256, 256, D_IDX, 256, 256, H_C)
    outs_bf = ((512, BF16), (512, BF16), (512, BF16), (512, BF16), (512, BF16), (256, BF16), (256, BF16),
               (1024, BF16), (LANES, BF16), (H_IDX, F32), (512, BF16), (256, BF16), (256, BF16), (H_C, F32))
    out_shape = tuple(jax.ShapeDtypeStruct((n, w), F32) for w in widths_f32) + tuple(
        jax.ShapeDtypeStruct((n, w), d) for w, d in outs_bf)
    out_specs = tuple(row(w) for w in widths_f32) + tuple(row(w) for w, _ in outs_bf)
    return pl.pallas_call(
        functools.partial(_proj_kernel, tiles_per_seq=t_seq // tm),
        grid=(n // tm,),
        in_specs=[
            row(D_MODEL),
            pl.BlockSpec((None, 1, D_MODEL), lambda i: (layer, 0, 0)),
            pl.BlockSpec((None, D_MODEL, W_MIX), lambda i: (layer, 0, 0), pipeline_mode=pl.Buffered(1)),
            pl.BlockSpec((None, 8, 512), lambda i: (layer, 0, 0)),
            _const_spec((512, 512), (0, 0)),
            tab, tab, tab,
            _const_spec((tm, tm), (0, 0)),
        ],
        out_specs=out_specs,
        out_shape=out_shape,
        scratch_shapes=[pltpu.VMEM((1, LANES), F32)],
        compiler_params=pltpu.CompilerParams(dimension_semantics=("arbitrary",),
                                             vmem_limit_bytes=VMEM_LIMIT),
        name="proj",
    )(x2d, p["norm_attn"], p["w_mix"], p["pvec"], p["bd"], cos, sa, sb, tri)


def _online_update(s, v, m_ref, l_ref, acc_ref, idx):
    m_prev = m_ref[idx]
    m_new = jnp.maximum(m_prev, jnp.max(s, axis=-1, keepdims=True))
    alpha = jnp.exp(m_prev - m_new)
    p = jnp.exp(s - m_new)
    l_ref[idx] = alpha * l_ref[idx] + jnp.sum(p, axis=-1, keepdims=True)
    acc_ref[idx] = alpha * acc_ref[idx] + _dot(p.astype(BF16), v)
    m_ref[idx] = m_new


def _attn_a_kernel(lamp_ref, sub_ref, q1_ref, q2_ref, k_ref, v_ref, o_ref, m_ref, l_ref, acc_ref,
                   *, lam_init):
    i = pl.program_id(1)
    j = pl.program_id(2)
    tq = q1_ref.shape[0]

    @pl.when(j == 0)
    def _():
        m_ref[...] = jnp.full_like(m_ref, NEG)
        l_ref[...] = jnp.zeros_like(l_ref)
        acc_ref[...] = jnp.zeros_like(acc_ref)

    def step(masked):
        if masked:
            r = lax.broadcasted_iota(jnp.int32, (tq, tq), 0)
            c = lax.broadcasted_iota(jnp.int32, (tq, tq), 1)
            keep = c <= r
        for h in range(H_A):
            sl = slice(h * LANES, (h + 1) * LANES)
            k = k_ref[:, sl]
            v = v_ref[:, sl]
            for comp, q_ref in enumerate((q1_ref, q2_ref)):
                s = _nt(q_ref[:, sl], k)
                if masked:
                    s = jnp.where(keep, s, NEG)
                _online_update(s, v, m_ref, l_ref, acc_ref, 2 * h + comp)

    @pl.when(j < i)
    def _():
        step(False)

    @pl.when(j == i)
    def _():
        step(True)
        lp = lamp_ref[...]
        lam = (jnp.exp(jnp.sum(lp[0:1] * lp[1:2], axis=-1, keepdims=True))
               - jnp.exp(jnp.sum(lp[2:3] * lp[3:4], axis=-1, keepdims=True)) + lam_init)
        for h in range(H_A):
            o = acc_ref[2 * h] / l_ref[2 * h] - lam * (acc_ref[2 * h + 1] / l_ref[2 * h + 1])
            y = o * lax.rsqrt(jnp.mean(o * o, axis=-1, keepdims=True) + EPS) * sub_ref[...]
            o_ref[:, h * LANES:(h + 1) * LANES] = (y * (1.0 - lam_init)).astype(o_ref.dtype)


def _attn_a_call(q1, q2, k, v, lamp, sub, layer, bsz, t, lam_init):
    tq = min(TQ_A, t)
    nq = t // tq
    qspec = pl.BlockSpec((tq, 512), lambda b, i, j: (b * nq + i, 0))
    kspec = pl.BlockSpec((tq, 512), lambda b, i, j: (b * nq + jnp.minimum(i, j), 0))
    return pl.pallas_call(
        functools.partial(_attn_a_kernel, lam_init=lam_init),
        grid=(bsz, nq, nq),
        in_specs=[pl.BlockSpec((None, 8, LANES), lambda b, i, j: (layer, 0, 0)),
                  pl.BlockSpec((None, 1, LANES), lambda b, i, j: (layer, 0, 0)),
                  qspec, qspec, kspec, kspec],
        out_specs=qspec,
        out_shape=jax.ShapeDtypeStruct((bsz * t, 512), BF16),
        scratch_shapes=[pltpu.VMEM((2 * H_A, tq, 1), F32), pltpu.VMEM((2 * H_A, tq, 1), F32),
                        pltpu.VMEM((2 * H_A, tq, LANES), F32)],
        compiler_params=pltpu.CompilerParams(
            dimension_semantics=("parallel", "parallel", "arbitrary"), vmem_limit_bytes=VMEM_LIMIT),
        name="attn_a",
    )(lamp, sub, q1, q2, k, v)


def _finish_pairs(o_ref, l_ref, acc_ref, n_heads):
    lane = lax.broadcasted_iota(jnp.int32, (1, LANES), 1)
    for jp in range(n_heads // 2):
        even = acc_ref[2 * jp] / l_ref[2 * jp]
        odd = acc_ref[2 * jp + 1] / l_ref[2 * jp + 1]
        o_ref[:, jp * LANES:(jp + 1) * LANES] = jnp.where(lane < HEAD, even, odd).astype(o_ref.dtype)


def _attn_c_kernel(q_ref, k_ref, v_ref, cq_ref, ckt_ref, o_ref, m_ref, l_ref, acc_ref):
    i = pl.program_id(1)
    j = pl.program_id(2)
    tq = q_ref.shape[0]

    @pl.when(j == 0)
    def _():
        m_ref[...] = jnp.full_like(m_ref, NEG)
        l_ref[...] = jnp.zeros_like(l_ref)
        acc_ref[...] = jnp.zeros_like(acc_ref)

    def step(masked):
        if masked:
            r = lax.broadcasted_iota(jnp.int32, (tq, tq), 0)
            c = lax.broadcasted_iota(jnp.int32, (tq, tq), 1)
            keep = c <= r
        cq = cq_ref[...]
        for h in range(H_C):
            pr = slice((h // 2) * LANES, (h // 2 + 1) * LANES)
            s = _nt(q_ref[:, h * LANES:(h + 1) * LANES], k_ref[:, pr])
            s = s + (cq[:, h:h + 1] - ckt_ref[h:h + 1, :])
            if masked:
                s = jnp.where(keep, s, NEG)
            _online_update(s, v_ref[:, pr], m_ref, l_ref, acc_ref, h)

    @pl.when(j < i)
    def _():
        step(False)

    @pl.when(j == i)
    def _():
        step(True)
        _finish_pairs(o_ref, l_ref, acc_ref, H_C)


def _attn_c_call(qz, k, v, cq, ckt, bsz, t):
    tq = min(TQ_C, t)
    nq = t // tq
    kv = lambda b, i, j: (b * nq + jnp.minimum(i, j), 0)
    return pl.pallas_call(
        _attn_c_kernel,
        grid=(bsz, nq, nq),
        in_specs=[pl.BlockSpec((tq, 512), lambda b, i, j: (b * nq + i, 0)),
                  pl.BlockSpec((tq, 256), kv), pl.BlockSpec((tq, 256), kv),
                  pl.BlockSpec((tq, H_C), lambda b, i, j: (b * nq + i, 0)),
                  pl.BlockSpec((None, 8, tq), lambda b, i, j: (b, 0, jnp.minimum(i, j)))],
        out_specs=pl.BlockSpec((tq, 256), lambda b, i, j: (b * nq + i, 0)),
        out_shape=jax.ShapeDtypeStruct((bsz * t, 256), BF16),
        scratch_shapes=[pltpu.VMEM((H_C, tq, 1), F32), pltpu.VMEM((H_C, tq, 1), F32),
                        pltpu.VMEM((H_C, tq, LANES), F32)],
        compiler_params=pltpu.CompilerParams(
            dimension_semantics=("parallel", "parallel", "arbitrary"), vmem_limit_bytes=VMEM_LIMIT),
        name="attn_c",
    )(qz, k, v, cq, ckt)


def _fold_lanes(m):
    part = m[:, 0:LANES]
    for u in range(1, m.shape[1] // LANES):
        part = part + m[:, u * LANES:(u + 1) * LANES]
    return part


def _mixer_b_kernel(iqz_ref, iw_ref, qz_ref, ik_ref, k_ref, v_ref, o_ref,
                    slab, lo_s, hi_s, cnt_s, midx_s, m_ref, l_ref, acc_ref, *, top_k, t_seq):
    i = pl.program_id(1)
    tq = qz_ref.shape[0]
    n_chunks = i + 1
    row_pos = i * tq + lax.broadcasted_iota(jnp.int32, (tq, 1), 0)
    n_causal = (row_pos + 1).astype(F32)
    kt = jnp.minimum(n_causal, float(top_k))
    iw = iw_ref[...]

    def scores(c):
        ik = ik_ref[pl.ds(pl.multiple_of(c * tq, tq), tq), :]
        acc = None
        for h in range(H_IDX):
            s = jnp.maximum(_nt(iqz_ref[:, h * LANES:(h + 1) * LANES], ik), 0.0) * iw[:, h:h + 1]
            acc = s if acc is None else acc + s
        return acc

    def fill(c, carry):
        mn, mx = carry
        s = scores(c)
        slab[c] = s
        return jnp.minimum(mn, _fold_lanes_min(s)), jnp.maximum(mx, _fold_lanes_max(s))

    big = jnp.full((tq, LANES), -NEG, F32)
    mn, mx = lax.fori_loop(0, i, fill, (big, -big))
    s = scores(i)
    r = lax.broadcasted_iota(jnp.int32, (tq, tq), 0)
    cl = lax.broadcasted_iota(jnp.int32, (tq, tq), 1)
    keep = cl <= r
    slab[i] = jnp.where(keep, s, NEG)
    mn = jnp.minimum(mn, _fold_lanes_min(jnp.where(keep, s, -NEG)))
    mx = jnp.maximum(mx, _fold_lanes_max(jnp.where(keep, s, NEG)))
    rowmin = jnp.min(mn, axis=1, keepdims=True)
    rowmax = jnp.max(mx, axis=1, keepdims=True)

    def count(pred):
        def body(c, acc):
            return acc + _fold_lanes(jnp.where(pred(slab[c], c), 1.0, 0.0))
        acc = lax.fori_loop(0, n_chunks, body, jnp.zeros((tq, LANES), F32))
        return jnp.sum(acc, axis=1, keepdims=True)

    c_max = count(lambda x, c: x >= rowmax)
    top_tied = c_max >= kt
    lo_s[...] = jnp.where(top_tied, rowmax, rowmin)
    hi_s[...] = rowmax
    cnt_s[...] = jnp.where(top_tied, c_max, n_causal)

    def probe():
        lo = lo_s[...]
        hi = hi_s[...]
        mid = lo + (hi - lo) * 0.5
        act = (cnt_s[...] != kt) & (mid > lo) & (mid < hi)
        return act, mid

    def n_active():
        act, _ = probe()
        return jnp.max(jnp.where(act, 1.0, 0.0))

    def bisect(_):
        act, mid = probe()
        c = count(lambda x, ch: x >= mid)
        up = act & (c >= kt)
        dn = act & (c < kt)
        lo_s[...] = jnp.where(up, mid, lo_s[...])
        cnt_s[...] = jnp.where(up, c, cnt_s[...])
        hi_s[...] = jnp.where(dn, mid, hi_s[...])
        return n_active()

    lax.while_loop(lambda n: n > 0.5, bisect, n_active())
    thr = lo_s[...]

    midx_s[...] = jnp.full_like(midx_s, float(t_seq))
    lane_f = lax.broadcasted_iota(jnp.int32, (1, tq), 1).astype(F32)

    def key_idx(c):
        return lane_f + (c * tq).astype(F32)

    @pl.when(jnp.max(jnp.where(cnt_s[...] > kt, 1.0, 0.0)) > 0.5)
    def _():
        need = kt - count(lambda x, c: x > thr)
        lo_s[...] = jnp.full_like(lo_s, -1.0)
        hi_s[...] = jnp.full_like(hi_s, float(t_seq - 1))

        def idx_step(_, carry):
            lo_i = lo_s[...]
            hi_i = hi_s[...]
            mid_i = jnp.floor((lo_i + hi_i) * 0.5)
            c = count(lambda x, ch: (x == thr) & (key_idx(ch) <= mid_i))
            ok = c >= need
            hi_s[...] = jnp.where(ok, mid_i, hi_i)
            lo_s[...] = jnp.where(ok, lo_i, mid_i)
            return carry

        lax.fori_loop(0, int(math.ceil(math.log2(t_seq))) + 1, idx_step, 0)
        midx_s[...] = jnp.where(cnt_s[...] > kt, hi_s[...], float(t_seq))

    midx = midx_s[...]

    m_ref[...] = jnp.full_like(m_ref, NEG)
    l_ref[...] = jnp.zeros_like(l_ref)
    acc_ref[...] = jnp.zeros_like(acc_ref)

    def attend(c, carry):
        x = slab[c]
        sel = (x > thr) | ((x == thr) & (key_idx(c) <= midx))
        bias = jnp.where(sel, 0.0, NEG)
        off = pl.multiple_of(c * tq, tq)
        for h in range(H_B):
            pr = slice((h // 2) * LANES, (h // 2 + 1) * LANES)
            sc = _nt(qz_ref[:, h * LANES:(h + 1) * LANES], k_ref[pl.ds(off, tq), pr]) + bias
            _online_update(sc, v_ref[pl.ds(off, tq), pr], m_ref, l_ref, acc_ref, h)
        return carry

    lax.fori_loop(0, n_chunks, attend, 0)
    _finish_pairs(o_ref, l_ref, acc_ref, H_B)


def _fold_lanes_min(m):
    part = m[:, 0:LANES]
    for u in range(1, m.shape[1] // LANES):
        part = jnp.minimum(part, m[:, u * LANES:(u + 1) * LANES])
    return part


def _fold_lanes_max(m):
    part = m[:, 0:LANES]
    for u in range(1, m.shape[1] // LANES):
        part = jnp.maximum(part, m[:, u * LANES:(u + 1) * LANES])
    return part


def _mixer_b_call(iqz, iw, qz, ik2, k, v, bsz, t):
    tq = min(TQ_B, t)
    nq = t // tq
    top_k = min(TOPK_MAX, t // 4)
    qrow = lambda w: pl.BlockSpec((tq, w), lambda b, i: (b * nq + i, 0))
    full = lambda w: pl.BlockSpec((t, w), lambda b, i: (b, 0))
    col = pltpu.VMEM((tq, 1), F32)
    return pl.pallas_call(
        functools.partial(_mixer_b_kernel, top_k=top_k, t_seq=t),
        grid=(bsz, nq),
        in_specs=[qrow(1024), qrow(H_IDX), qrow(512), full(LANES), full(256), full(256)],
        out_specs=qrow(256),
        out_shape=jax.ShapeDtypeStruct((bsz * t, 256), BF16),
        scratch_shapes=[pltpu.VMEM((nq, tq, tq), F32), col, col, col, col,
                        pltpu.VMEM((H_B, tq, 1), F32), pltpu.VMEM((H_B, tq, 1), F32),
                        pltpu.VMEM((H_B, tq, LANES), F32)],
        compiler_params=pltpu.CompilerParams(
            dimension_semantics=("parallel", "arbitrary"), vmem_limit_bytes=VMEM_LIMIT),
        name="mixer_b",
    )(iqz, iw, qz, ik2, k, v)


def _merge_kernel(x_ref, g_ref, oa_ref, ob_ref, oc_ref, wg_ref, wa_ref, wb_ref, wc_ref, wo_ref, h_ref):
    x = x_ref[...]
    ms = jnp.mean(x * x, axis=-1, keepdims=True)
    xn = (x * lax.rsqrt(ms + EPS) * g_ref[...]).astype(BF16)
    merged = None
    for br, (o_ref, w_ref) in enumerate(((oa_ref, wa_ref), (ob_ref, wb_ref), (oc_ref, wc_ref))):
        gate = jax.nn.sigmoid(_dot(xn, wg_ref[:, br * D_MODEL:(br + 1) * D_MODEL]))
        term = gate * _dot(o_ref[...], w_ref[...])
        merged = term if merged is None else merged + term
    h_ref[...] = x + _dot(merged.astype(BF16), wo_ref[...])


def _merge_call(x2d, oa, ob, oc, layer, p):
    n = x2d.shape[0]
    tm = min(TM_MERGE, n)
    row = lambda w: pl.BlockSpec((tm, w), lambda i: (i, 0))
    wspec = lambda r, c: pl.BlockSpec((None, r, c), lambda i: (layer, 0, 0), pipeline_mode=pl.Buffered(1))
    return pl.pallas_call(
        _merge_kernel,
        grid=(n // tm,),
        in_specs=[row(D_MODEL), pl.BlockSpec((None, 1, D_MODEL), lambda i: (layer, 0, 0)),
                  row(512), row(256), row(256),
                  wspec(D_MODEL, N_BRANCH * D_MODEL), wspec(512, D_MODEL), wspec(256, D_MODEL),
                  wspec(256, D_MODEL), wspec(D_MODEL, D_MODEL)],
        out_specs=row(D_MODEL),
        out_shape=jax.ShapeDtypeStruct((n, D_MODEL), F32),
        compiler_params=pltpu.CompilerParams(dimension_semantics=("parallel",),
                                             vmem_limit_bytes=VMEM_LIMIT),
        name="merge",
    )(x2d, p["norm_attn"], oa, ob, oc, p["w_g"], p["w_out_a"], p["w_out_b"], p["w_out_c"], p["w_o"])


def _ffn_kernel(h_ref, g_ref, wg_ref, wu_ref, wd_ref, y_ref):
    h = h_ref[...]
    ms = jnp.mean(h * h, axis=-1, keepdims=True)
    hn = (h * lax.rsqrt(ms + EPS) * g_ref[...]).astype(BF16)
    gate = _dot(hn, wg_ref[...])
    up = _dot(hn, wu_ref[...])
    act = (gate * jax.nn.sigmoid(gate) * up).astype(BF16)
    y_ref[...] = h + _dot(act, wd_ref[...])


def _ffn_call(h2d, layer, p):
    n = h2d.shape[0]
    tm = min(TM_FFN, n)
    d_ff = p["w_gate"].shape[-1]
    row = lambda w: pl.BlockSpec((tm, w), lambda i: (i, 0))
    wspec = lambda r, c: pl.BlockSpec((None, r, c), lambda i: (layer, 0, 0), pipeline_mode=pl.Buffered(1))
    return pl.pallas_call(
        _ffn_kernel,
        grid=(n // tm,),
        in_specs=[row(D_MODEL), pl.BlockSpec((None, 1, D_MODEL), lambda i: (layer, 0, 0)),
                  wspec(D_MODEL, d_ff), wspec(D_MODEL, d_ff), wspec(d_ff, D_MODEL)],
        out_specs=row(D_MODEL),
        out_shape=jax.ShapeDtypeStruct((n, D_MODEL), F32),
        compiler_params=pltpu.CompilerParams(dimension_semantics=("parallel",),
                                             vmem_limit_bytes=VMEM_LIMIT),
        name="ffn",
    )(h2d, p["norm_ffn"], p["w_gate"], p["w_up"], p["w_down"])


QROWS = 16


def _decode_kernel(pt_ref, lamp_ref, sub_ref,
                   qa_ref, qb_ref, qi_ref, qc_ref, iw_ref,
                   nak_ref, nav_ref, nbk_ref, nbv_ref, nik_ref, nck_ref, ncv_ref, nlf_ref,
                   ak_ref, av_ref, bk_ref, bv_ref, ki_ref, ck_ref, cv_ref, lft_ref, tri_ref,
                   oa_ref, ob_ref, oc_ref,
                   ma, la, acca, mc, lc, accc, carry, isc_s, sb_s, vb_s, *, lam_init, top_k):
    j = pl.program_id(1)
    n_pages = pl.num_programs(1)
    page = n_pages - 1 - j
    psz = lft_ref.shape[1]
    qa = qa_ref[...]
    qb = qb_ref[...]
    qi = qi_ref[...]
    qc = qc_ref[...]
    iw = iw_ref[...]
    lane = lax.broadcasted_iota(jnp.int32, (1, psz), 1)

    def rowdot(q, krow):
        return jnp.sum(q.astype(F32) * krow.astype(F32), axis=-1, keepdims=True)

    @pl.when(j == 0)
    def _():
        ma[...] = rowdot(qa, nak_ref[...])
        la[...] = jnp.ones_like(la)
        acca[...] = nav_ref[...].astype(F32)
        mc[...] = rowdot(qc, nck_ref[...])
        lc[...] = jnp.ones_like(lc)
        accc[...] = jnp.broadcast_to(ncv_ref[...].astype(F32), accc.shape)
        carry[...] = jnp.broadcast_to(nlf_ref[...], carry.shape)
        own_i = jnp.sum(jnp.maximum(rowdot(qi, nik_ref[...]), 0.0) * iw, axis=0, keepdims=True)
        isc_s[...] = jnp.full_like(isc_s, NEG)
        isc_s[pl.ds(n_pages, 1), :] = jnp.where(lane == 0, own_i, NEG)
        sb_s[n_pages] = jnp.broadcast_to(rowdot(qb, nbk_ref[...]), (QROWS, psz))
        first = jnp.where(lane == 0, 1.0, 0.0).astype(F32)
        vb_s[n_pages] = (nbv_ref[...].astype(F32) * first).astype(BF16)

    def online(s, pv, m_ref, l_ref, acc_ref):
        m_prev = m_ref[...]
        m_new = jnp.maximum(m_prev, jnp.max(s, axis=-1, keepdims=True))
        alpha = jnp.exp(m_prev - m_new)
        p = jnp.exp(s - m_new)
        l_ref[...] = alpha * l_ref[...] + jnp.sum(p, axis=-1, keepdims=True)
        acc_ref[...] = alpha * acc_ref[...] + pv(p.astype(BF16))
        m_ref[...] = m_new

    n_rows = ak_ref.shape[0]
    key_head = lax.broadcasted_iota(jnp.int32, (QROWS, n_rows), 1) % H_A
    row_head = lax.broadcasted_iota(jnp.int32, (QROWS, n_rows), 0) // 2
    head_bias = jnp.where(key_head == row_head, 0.0, NEG)
    av = av_ref[...].astype(BF16)
    online(_nt(qa, ak_ref[...].astype(BF16)) + head_bias, lambda p: _dot(p, av), ma, la, acca)
    lft = jnp.concatenate([lft_ref[...], jnp.zeros((8 - H_C, psz), F32)], axis=0)
    suffix = jnp.dot(lft, tri_ref[...], preferred_element_type=F32,
                     precision=lax.Precision.HIGHEST) + carry[...][0:8]
    bias_c = jnp.concatenate([suffix, jnp.zeros((QROWS - 8, psz), F32)], axis=0)
    cv = cv_ref[...].astype(BF16)
    online(_dot(qc, ck_ref[...].astype(BF16)) + bias_c, lambda p: _nt(p, cv), mc, lc, accc)
    carry[0:8] = carry[0:8] + jnp.sum(lft, axis=-1, keepdims=True)
    si = jnp.maximum(_dot(qi, ki_ref[...].astype(BF16)), 0.0) * iw
    isc_s[pl.ds(page, 1), :] = jnp.sum(si, axis=0, keepdims=True)
    sb_s[page] = _dot(qb, bk_ref[...].astype(BF16))
    vb_s[page] = bv_ref[...].astype(BF16)

    @pl.when(j == n_pages - 1)
    def _():
        lp = lamp_ref[...]
        lam = (jnp.exp(jnp.sum(lp[0:1] * lp[1:2], axis=-1, keepdims=True))
               - jnp.exp(jnp.sum(lp[2:3] * lp[3:4], axis=-1, keepdims=True)) + lam_init)
        oa = acca[...] / la[...]
        for h in range(H_A):
            sl = slice(h * LANES, (h + 1) * LANES)
            o = oa[2 * h:2 * h + 1, :] - lam * oa[2 * h + 1:2 * h + 2, :]
            y = o * lax.rsqrt(jnp.mean(o * o, axis=-1, keepdims=True) + EPS) * sub_ref[...]
            oa_ref[:, sl] = (y * (1.0 - lam_init)).astype(oa_ref.dtype)
        oc = accc[...] / lc[...]
        for h in range(H_C):
            oc_ref[:, h * HEAD:(h + 1) * HEAD] = oc[h:h + 1, h * HEAD:(h + 1) * HEAD].astype(oc_ref.dtype)

        x = isc_s[...]
        valid = x > NEG * 0.5
        kt = float(top_k)
        idx = (lax.broadcasted_iota(jnp.int32, x.shape, 0) * psz
               + lax.broadcasted_iota(jnp.int32, x.shape, 1)).astype(F32)

        def count(pred):
            return jnp.sum(jnp.where(pred, 1.0, 0.0))

        vmax = jnp.max(x)
        vmin = jnp.min(jnp.where(valid, x, -NEG))
        c_max = count(x >= vmax)
        top_tied = c_max >= kt
        lo0 = jnp.where(top_tied, vmax, vmin)
        cnt0 = jnp.where(top_tied, c_max, count(valid))

        def active(lo, hi, cnt):
            mid = lo + (hi - lo) * 0.5
            return (cnt != kt) & (mid > lo) & (mid < hi)

        def bisect(st):
            lo, hi, cnt = st
            mid = lo + (hi - lo) * 0.5
            c = count(x >= mid)
            up = c >= kt
            return jnp.where(up, mid, lo), jnp.where(up, hi, mid), jnp.where(up, c, cnt)

        thr, _, cnt = lax.while_loop(lambda st: active(*st), bisect, (lo0, vmax, cnt0))
        need = kt - count(x > thr)
        n_keys = float(x.shape[0] * psz)

        def idx_step(_, st):
            lo_i, hi_i = st
            mid_i = jnp.floor((lo_i + hi_i) * 0.5)
            ok = count((x == thr) & (idx <= mid_i)) >= need
            return jnp.where(ok, lo_i, mid_i), jnp.where(ok, mid_i, hi_i)

        _, midx = lax.fori_loop(0, int(math.ceil(math.log2(n_keys))) + 1, idx_step,
                                (jnp.float32(-1.0), jnp.float32(n_keys - 1.0)))
        sel = (x > thr) | ((x == thr) & (idx <= midx))
        bias = jnp.where(sel & valid, 0.0, NEG)
        isc_s[...] = bias

        def pmax(p, m):
            return jnp.maximum(m, jnp.max(sb_s[p] + isc_s[pl.ds(p, 1), :], axis=-1, keepdims=True))

        mb = lax.fori_loop(0, n_pages + 1, pmax, jnp.full((QROWS, 1), NEG, F32))

        def pacc(p, st):
            l, acc = st
            pr = jnp.exp(sb_s[p] + isc_s[pl.ds(p, 1), :] - mb)
            return l + jnp.sum(pr, axis=-1, keepdims=True), acc + _nt(pr.astype(BF16), vb_s[p])

        lb, accb = lax.fori_loop(0, n_pages + 1, pacc,
                                 (jnp.zeros((QROWS, 1), F32), jnp.zeros((QROWS, H_B * D_B), F32)))
        ob = accb / lb
        for h in range(H_B):
            ob_ref[:, h * HEAD:(h + 1) * HEAD] = ob[h:h + 1, h * HEAD:(h + 1) * HEAD].astype(ob_ref.dtype)


def _decode_call(page_table, layer, lam_init, qmats, new_rows, caches, p):
    db, n_pages = page_table.shape
    lft = caches[-1]
    psz = lft.shape[3]
    top_k = min(TOPK_MAX, (n_pages * psz + 1) // 4)
    tri = jnp.triu(jnp.ones((psz, psz), F32), 1).T
    qa, qb, qi, qc, iw = qmats

    def per_b(a):
        return pl.BlockSpec((None,) + a.shape[1:], lambda b, j, pt: (b,) + (0,) * (a.ndim - 1))

    def paged(a):
        return pl.BlockSpec((None, None) + a.shape[2:],
                            lambda b, j, pt: (layer, pt[b, n_pages - 1 - j], 0, 0))

    in_specs = ([pl.BlockSpec((None, 8, LANES), lambda b, j, pt: (layer, 0, 0)),
                 pl.BlockSpec((None, 1, LANES), lambda b, j, pt: (layer, 0, 0))]
                + [per_b(a) for a in (qa, qb, qi, qc, iw)]
                + [per_b(a) for a in new_rows]
                + [paged(a) for a in caches]
                + [pl.BlockSpec((psz, psz), lambda b, j, pt: (0, 0))])
    out = lambda w: pl.BlockSpec((None, 1, w), lambda b, j, pt: (b, 0, 0))
    grid_spec = pltpu.PrefetchScalarGridSpec(
        num_scalar_prefetch=1, grid=(db, n_pages), in_specs=in_specs,
        out_specs=(out(512), out(256), out(256)),
        scratch_shapes=[pltpu.VMEM((QROWS, 1), F32), pltpu.VMEM((QROWS, 1), F32), pltpu.VMEM((QROWS, LANES), F32),
                        pltpu.VMEM((QROWS, 1), F32), pltpu.VMEM((QROWS, 1), F32), pltpu.VMEM((QROWS, 256), F32),
                        pltpu.VMEM((QROWS, 1), F32),
                        pltpu.VMEM((-(-(n_pages + 1) // 8) * 8, psz), F32),
                        pltpu.VMEM((n_pages + 1, QROWS, psz), F32),
                        pltpu.VMEM((n_pages + 1, H_B * D_B, psz), BF16)])
    return pl.pallas_call(
        functools.partial(_decode_kernel, lam_init=lam_init, top_k=top_k),
        grid_spec=grid_spec,
        out_shape=(jax.ShapeDtypeStruct((db, 1, 512), BF16), jax.ShapeDtypeStruct((db, 1, 256), BF16),
                   jax.ShapeDtypeStruct((db, 1, 256), BF16)),
        compiler_params=pltpu.CompilerParams(dimension_semantics=("parallel", "arbitrary"),
                                             vmem_limit_bytes=VMEM_LIMIT),
        name="decode",
    )(page_table, p["lamp"], p["subln_a"], qa, qb, qi, qc, iw, *new_rows, *caches, tri)


def _rope_tables(pos):
    r = HEAD // ROT_FRAC
    half = r // 2
    inv = jnp.power(ROPE_THETA, -jnp.arange(half, dtype=F32) * 2.0 / r)
    ang = pos.astype(F32)[:, None] * inv[None, :]
    cos, sin = jnp.cos(ang), jnp.sin(ang)
    n = pos.shape[0]
    z = lambda w: jnp.zeros((n, w), F32)
    cos_p = jnp.concatenate([cos, cos, jnp.ones((n, HEAD - r), F32)], axis=1)
    sa_p = jnp.concatenate([-sin, z(HEAD - half)], axis=1)
    sb_p = jnp.concatenate([z(half), sin, z(HEAD - r)], axis=1)
    return tuple(jnp.tile(t, (1, LANES // HEAD)) for t in (cos_p, sa_p, sb_p))


def _prep_params(norm_attn, w_in, b_forget, q_norm_a, k_norm_a, lambda_q1, lambda_k1, lambda_q2, lambda_k2,
                 subln_a, q_norm_b, k_norm_b, k_norm_idx, q_norm_c, k_norm_c, w_out_a, w_out_b, w_out_c,
                 w_o, norm_ffn, w_gate, w_up, w_down):
    depth = w_in.shape[0]
    (aq, ak, av, bq, bk, bv, iq, ik, iw, cq, ck, cv, cf, gl) = jnp.split(
        w_in, np.cumsum(IN_WIDTHS)[:-1].tolist(), axis=-1)
    misc = jnp.concatenate([ik, iw, cf, jnp.zeros((depth, D_MODEL, LANES - D_IDX - H_IDX - H_C), F32)], axis=-1)
    w_mix = jnp.concatenate([aq, ak, av, bq, bk, bv, iq, misc, cq, ck, cv], axis=-1).astype(BF16)

    def row(v, reps, width=512, fill=1.0):
        t = jnp.tile(v, (1, reps))
        return jnp.concatenate([t, jnp.full((depth, width - t.shape[1]), fill, F32)], axis=1)

    misc_gain = jnp.concatenate([k_norm_idx, jnp.ones((depth, 512 - D_IDX), F32)], axis=1)
    bf_row = jnp.concatenate([jnp.zeros((depth, MISC_CF), F32), b_forget,
                              jnp.zeros((depth, 512 - MISC_CF - H_C), F32)], axis=1)
    pvec = jnp.stack([row(q_norm_a, 8), row(k_norm_a, 8), row(q_norm_b, 4), row(k_norm_b, 4),
                      row(q_norm_c, 4), row(k_norm_c, 4), misc_gain, bf_row], axis=1)
    zpad = jnp.zeros((depth, LANES - DK_A), F32)
    lam_rows = [jnp.concatenate([v, zpad], axis=1) for v in (lambda_q1, lambda_k1, lambda_q2, lambda_k2)]
    lamp = jnp.stack(lam_rows + [jnp.zeros((depth, LANES), F32)] * 4, axis=1)
    blk = np.kron(np.eye(512 // HEAD), np.ones((HEAD, HEAD))) / HEAD
    return dict(
        norm_attn=norm_attn[:, None, :], w_mix=w_mix, w_g=gl.astype(BF16), pvec=pvec,
        bd=jnp.asarray(blk, BF16), lamp=lamp, subln_a=subln_a[:, None, :],
        w_out_a=w_out_a.astype(BF16), w_out_b=w_out_b.astype(BF16), w_out_c=w_out_c.astype(BF16),
        w_o=w_o.astype(BF16), norm_ffn=norm_ffn[:, None, :],
        w_gate=w_gate.astype(BF16), w_up=w_up.astype(BF16), w_down=w_down.astype(BF16))


def _pair_rows(even, odd):
    db, h, w = even.shape
    rows = jnp.stack([even, odd], axis=2).reshape(db, 2 * h, w)
    return jnp.concatenate([rows, jnp.zeros((db, QROWS - 2 * h, w), rows.dtype)], axis=1)


def _decode_qmats(aq1z, aq2z, bqz, iqz, cqz, iw):
    db = aq1z.shape[0]
    qa = _pair_rows(aq1z.reshape(db, H_A, LANES), aq2z.reshape(db, H_A, LANES))

    def unpad(qz, n_heads):
        return qz.reshape(db, n_heads // 2, 2, LANES).sum(axis=2).reshape(db, n_heads * HEAD)

    def head_rows(q, n_heads):
        m = np.zeros((QROWS, n_heads * HEAD), np.float32)
        for h in range(n_heads):
            m[h, h * HEAD:(h + 1) * HEAD] = 1.0
        return q[:, None, :] * jnp.asarray(m, BF16)

    qb = head_rows(unpad(bqz, H_B), H_B)
    qc = head_rows(unpad(cqz, H_C), H_C)
    iq = unpad(iqz, H_IDX).reshape(db, H_IDX, HEAD)
    qi = jnp.concatenate([iq, jnp.zeros((db, QROWS - H_IDX, HEAD), BF16)], axis=1)
    iwc = jnp.concatenate([iw, jnp.zeros((db, QROWS - H_IDX), F32)], axis=1)[:, :, None]
    return qa, qb, qi, qc, iwc


def kernel(x_prompt, x_sample, cache_a_k, cache_a_v, cache_b_k, cache_b_v, cache_b_kidx, cache_c_k, cache_c_v, cache_c_logf, page_table, norm_attn, w_in, b_forget, q_norm_a, k_norm_a, lambda_q1, lambda_k1, lambda_q2, lambda_k2, subln_a, q_norm_b, k_norm_b, k_norm_idx, q_norm_c, k_norm_c, w_out_a, w_out_b, w_out_c, w_o, norm_ffn, w_gate, w_up, w_down):
    bsz, t, _ = x_prompt.shape
    db, ds, _ = x_sample.shape
    assert ds == 1
    depth = w_in.shape[0]
    n_pool, psz = cache_a_k.shape[1], cache_a_k.shape[2]
    past_len = page_table.shape[1] * psz
    p = _prep_params(norm_attn, w_in, b_forget, q_norm_a, k_norm_a, lambda_q1, lambda_k1, lambda_q2,
                     lambda_k2, subln_a, q_norm_b, k_norm_b, k_norm_idx, q_norm_c, k_norm_c,
                     w_out_a, w_out_b, w_out_c, w_o, norm_ffn, w_gate, w_up, w_down)
    tabs_p = _rope_tables(jnp.arange(t, dtype=jnp.int32))
    tabs_s = _rope_tables(jnp.full((db,), past_len, jnp.int32))
    rows_view = lambda c: c.reshape(depth, n_pool, psz * c.shape[3], c.shape[4])
    tr_view = lambda c: jnp.transpose(c, (0, 1, 3, 4, 2)).reshape(depth, n_pool, c.shape[3] * c.shape[4], psz)
    caches = (rows_view(cache_a_k), rows_view(cache_a_v), tr_view(cache_b_k), tr_view(cache_b_v),
              jnp.swapaxes(cache_b_kidx, 2, 3), tr_view(cache_c_k), tr_view(cache_c_v),
              jnp.swapaxes(cache_c_logf, 2, 3))

    xp = x_prompt.reshape(bsz * t, D_MODEL)
    xs = x_sample.reshape(db, D_MODEL)
    rows_p, rows_s = [], []
    for l in range(depth):
        lam_init = 0.8 - 0.6 * math.exp(-0.3 * l)
        (ak, av, bk, bv, ik, ck, cv, lf, aq1, aq2, akb, avb, bqz, bkb, bvb, iqz, ik2, iw, cqz, ckb, cvb,
         cc) = _proj_call(xp, l, p, tabs_p, t)
        rows_p.append((ak, av, bk, bv, ik, ck, cv, lf))
        oa = _attn_a_call(aq1, aq2, akb, avb, p["lamp"], p["subln_a"], l, bsz, t, lam_init)
        ob = _mixer_b_call(iqz, iw, bqz, ik2, bkb, bvb, bsz, t)
        cct = jnp.swapaxes(cc.reshape(bsz, t, H_C), 1, 2)
        cct = jnp.concatenate([cct, jnp.zeros((bsz, 8 - H_C, t), F32)], axis=1)
        oc = _attn_c_call(cqz, ckb, cvb, cc, cct, bsz, t)
        xp = _ffn_call(_merge_call(xp, oa, ob, oc, l, p), l, p)
        (ak, av, bk, bv, ik, ck, cv, lf, aq1, aq2, akb, avb, bqz, bkb, bvb, iqz, ik2, iw, cqz, ckb, cvb,
         _) = _proj_call(xs, l, p, tabs_s, db)
        rows_s.append((ak, av, bk, bv, ik, ck, cv, lf))
        qmats = _decode_qmats(aq1, aq2, bqz, iqz, cqz, iw)
        lf8 = jnp.concatenate([lf, jnp.zeros((db, QROWS - H_C), F32)], axis=1)[:, :, None]
        ak_h = akb.reshape(db, H_A, LANES)
        av_h = avb.reshape(db, H_A, LANES)
        new_rows = (_pair_rows(ak_h, ak_h), _pair_rows(av_h, av_h), bkb[:, None, :],
                    bvb.astype(F32)[:, :, None], ik2[:, None, 0:D_IDX], ckb[:, None, :], cvb[:, None, :], lf8)
        oa, ob, oc = _decode_call(page_table, l, lam_init, qmats, new_rows, caches, p)
        xs = _ffn_call(_merge_call(xs, oa.reshape(db, 512), ob.reshape(db, 256), oc.reshape(db, 256), l, p),
                       l, p)

    def stack(rows, idx, lead, tail):
        return jnp.stack([r[idx] for r in rows], axis=0).reshape((depth,) + lead + tail)

    tails = ((H_A, 2 * DK_A), (H_A, DV_A), (H_B, D_B), (H_B, D_B), (D_IDX,), (H_C, D_C), (H_C, D_C), (H_C,))
    outs_p = [stack(rows_p, i, (bsz, t), tails[i]) for i in range(8)]
    outs_s = [stack(rows_s, i, (db, 1), tails[i]) for i in range(8)]
    return (xp.reshape(bsz, t, D_MODEL), xs.reshape(db, 1, D_MODEL), *outs_p, *outs_s)
```

```python
import functools
import math

import numpy as np
import jax
import jax.numpy as jnp
from jax import lax
from jax.experimental import pallas as pl
from jax.experimental.pallas import tpu as pltpu

F32 = jnp.float32
BF16 = jnp.bfloat16

D_MODEL = 1024
H_A, DK_A, DV_A = 4, 64, 128
H_B, D_B = 4, 64
H_IDX, D_IDX = 8, 64
TOPK_MAX = 256
H_C, D_C = 4, 64
N_BRANCH = 3
ROPE_THETA = 500000.0
ROT_FRAC = 4
EPS = 1e-6
NEG = -1e30
IN_WIDTHS = (H_A * 2 * DK_A, H_A * 2 * DK_A, H_A * DV_A,
             H_B * D_B, H_B * D_B, H_B * D_B,
             H_IDX * D_IDX, D_IDX, H_IDX,
             H_C * D_C, H_C * D_C, H_C * D_C, H_C,
             N_BRANCH * D_MODEL)

LANES = 128
HEAD = 64
QK_SCALE = HEAD ** -0.5

OFF_AQ, OFF_AK, OFF_AV = 0, 512, 1024
OFF_BQ, OFF_BK, OFF_BV = 1536, 1792, 2048
OFF_IQ, OFF_MISC = 2304, 2816
OFF_CQ, OFF_CK, OFF_CV = 2944, 3200, 3456
W_MIX = 3712
MISC_IW, MISC_CF = 64, 72

TM_PROJ = 512
TQ_A = 512
TQ_C = 512
TQ_B = 512
TM_MERGE = 512
TM_FFN = 512
VMEM_LIMIT = 56 * 1024 * 1024


def _nt(a, b):
    return lax.dot_general(a, b, (((1,), (1,)), ((), ())), preferred_element_type=F32)


def _dot(a, b):
    return jnp.dot(a, b, preferred_element_type=F32)


def _const_spec(shape, index):
    n = len(index)
    return pl.BlockSpec(shape, lambda *_: index, pipeline_mode=pl.Buffered(1))


def _proj_kernel(x_ref, g_ref, w_ref, pv_ref, bd_ref, cos_ref, sa_ref, sb_ref, tri_ref,
                 ak_o, av_o, bk_o, bv_o, ik_o, ck_o, cv_o, lf_o,
                 aq1_o, aq2_o, akb_o, avb_o, bqz_o, bkb_o, bvb_o, iqz_o, ik2_o, iw_o,
                 cqz_o, ckb_o, cvb_o, cc_o, carry_ref, *, tiles_per_seq, transposed_v):
    i = pl.program_id(0)
    x = x_ref[...]
    ms = jnp.mean(x * x, axis=-1, keepdims=True)
    xn = (x * lax.rsqrt(ms + EPS) * g_ref[...]).astype(BF16)
    cos = cos_ref[...]
    sa = sa_ref[...]
    sb = sb_ref[...]
    lane = lax.broadcasted_iota(jnp.int32, (1, LANES), 1)
    lo = jnp.where(lane < HEAD, 1.0, 0.0).astype(F32)
    hi = 1.0 - lo

    def mm(off, w):
        return _dot(xn, w_ref[:, off:off + w])

    def tile(t, w):
        return t if w == LANES else jnp.concatenate([t] * (w // LANES), axis=1)

    def norm(p, row, w):
        x2 = p * p
        h = x2.astype(BF16)
        l = (x2 - h.astype(F32)).astype(BF16)
        bd = bd_ref[0:w, 0:w]
        m = _dot(h, bd) + _dot(l, bd)
        return p * lax.rsqrt(m + EPS) * pv_ref[row:row + 1, 0:w]

    def rope(y, w):
        return (y * tile(cos, w) + pltpu.roll(y, w - 8, 1) * tile(sa, w)
                + pltpu.roll(y, 8, 1) * tile(sb, w))

    def vcast(v):
        return (v.T if transposed_v else v).astype(BF16)

    def pad_heads(y, n_pairs):
        parts = []
        for j in range(n_pairs):
            col = y[:, j * LANES:(j + 1) * LANES]
            parts.append(col * lo)
            parts.append(col * hi)
        return jnp.concatenate(parts, axis=1)

    aq = rope(norm(mm(OFF_AQ, 512), 0, 512), 512) * QK_SCALE
    aq1_o[...] = (aq * tile(lo, 512)).astype(BF16)
    aq2_o[...] = (aq * tile(hi, 512)).astype(BF16)
    ak = rope(norm(mm(OFF_AK, 512), 1, 512), 512)
    ak_o[...] = ak
    akb_o[...] = ak.astype(BF16)
    av = mm(OFF_AV, 512)
    av_o[...] = av
    avb_o[...] = vcast(av)
    bq = rope(norm(mm(OFF_BQ, 256), 2, 256), 256) * QK_SCALE
    bqz_o[...] = pad_heads(bq, 2).astype(BF16)
    bk = rope(norm(mm(OFF_BK, 256), 3, 256), 256)
    bk_o[...] = bk
    bkb_o[...] = bk.astype(BF16)
    bv = mm(OFF_BV, 256)
    bv_o[...] = bv
    bvb_o[...] = vcast(bv)
    iq = rope(mm(OFF_IQ, 512), 512) * QK_SCALE
    iqz_o[...] = pad_heads(iq, 4).astype(BF16)
    misc = mm(OFF_MISC, LANES)
    ik = rope(norm(misc, 6, LANES), LANES)
    ik_o[...] = ik[:, 0:D_IDX]
    ik2_o[...] = jnp.where(lane < HEAD, ik, pltpu.roll(ik, HEAD, 1)).astype(BF16)
    iw_o[...] = misc[:, MISC_IW:MISC_IW + H_IDX] * (H_IDX ** -0.5)
    cq = norm(mm(OFF_CQ, 256), 4, 256) * QK_SCALE
    cqz_o[...] = pad_heads(cq, 2).astype(BF16)
    ck = norm(mm(OFF_CK, 256), 5, 256)
    ck_o[...] = ck
    ckb_o[...] = ck.astype(BF16)
    cv = mm(OFF_CV, 256)
    cv_o[...] = cv
    cvb_o[...] = vcast(cv)
    z = misc + pv_ref[7:8, 0:LANES]
    lf = jnp.minimum(z, 0.0) - jnp.log1p(jnp.exp(-jnp.abs(z)))
    lf_o[...] = lf[:, MISC_CF:MISC_CF + H_C]

    @pl.when(i % tiles_per_seq == 0)
    def _():
        carry_ref[...] = jnp.zeros_like(carry_ref)

    cum = jnp.dot(tri_ref[...], lf, preferred_element_type=F32,
                  precision=lax.Precision.HIGHEST) + carry_ref[...]
    cc_o[...] = cum[:, MISC_CF:MISC_CF + H_C]
    tm = cum.shape[0]
    carry_ref[...] = cum[tm - 1:tm, :]


def _proj_call(x2d, layer, p, tabs, t_seq, transposed_v):
    n = x2d.shape[0]
    tm = min(TM_PROJ, n)
    assert n % tm == 0 and t_seq % tm == 0
    cos, sa, sb = tabs
    ntab = cos.shape[0] // tm
    tri = jnp.tril(jnp.ones((tm, tm), F32))
    row = lambda w: pl.BlockSpec((tm, w), lambda i: (i, 0))
    tab = pl.BlockSpec((tm, LANES), lambda i: (i % ntab, 0))
    widths_f32 = (512, 512, 256, 256, D_IDX, 256, 256, H_C)
    outs_bf = ((512, BF16), (512, BF16), (512, BF16), (512, BF16), (512, BF16), (256, BF16), (256, BF16),
               (1024, BF16), (LANES, BF16), (H_IDX, F32), (512, BF16), (256, BF16), (256, BF16), (H_C, F32))
    v_slots = (3, 6, 12) if transposed_v else ()
    out_shape = tuple(jax.ShapeDtypeStruct((n, w), F32) for w in widths_f32) + tuple(
        jax.ShapeDtypeStruct((n // tm, w, tm) if s in v_slots else (n, w), d) for s, (w, d) in enumerate(outs_bf))
    out_specs = tuple(row(w) for w in widths_f32) + tuple(
        pl.BlockSpec((None, w, tm), lambda i: (i, 0, 0)) if s in v_slots else row(w)
        for s, (w, _) in enumerate(outs_bf))
    return pl.pallas_call(
        functools.partial(_proj_kernel, tiles_per_seq=t_seq // tm, transposed_v=transposed_v),
        grid=(n // tm,),
        in_specs=[
            row(D_MODEL),
            pl.BlockSpec((None, 1, D_MODEL), lambda i: (layer, 0, 0)),
            pl.BlockSpec((None, D_MODEL, W_MIX), lambda i: (layer, 0, 0), pipeline_mode=pl.Buffered(1)),
            pl.BlockSpec((None, 8, 512), lambda i: (layer, 0, 0)),
            _const_spec((512, 512), (0, 0)),
            tab, tab, tab,
            _const_spec((tm, tm), (0, 0)),
        ],
        out_specs=out_specs,
        out_shape=out_shape,
        scratch_shapes=[pltpu.VMEM((1, LANES), F32)],
        compiler_params=pltpu.CompilerParams(dimension_semantics=("arbitrary",),
                                             vmem_limit_bytes=VMEM_LIMIT),
        name="proj",
    )(x2d, p["norm_attn"], p["w_mix"], p["pvec"], p["bd"], cos, sa, sb, tri)


def _online_update(s_t, v_t, m_ref, l_ref, acc_ref, idx):
    m_prev = m_ref[idx]
    m_new = jnp.maximum(m_prev, jnp.max(s_t, axis=0, keepdims=True))
    alpha = jnp.exp(m_prev - m_new)
    p = jnp.exp(s_t - m_new)
    l_ref[idx] = alpha * l_ref[idx] + jnp.sum(p, axis=0, keepdims=True)
    acc_ref[idx] = alpha * acc_ref[idx] + _dot(v_t, p.astype(BF16))
    m_ref[idx] = m_new


def _init_stats(m_ref, l_ref, acc_ref):
    m_ref[...] = jnp.full_like(m_ref, NEG)
    l_ref[...] = jnp.zeros_like(l_ref)
    acc_ref[...] = jnp.zeros_like(acc_ref)


def _causal_keep(tk, tq):
    return (lax.broadcasted_iota(jnp.int32, (tk, tq), 0) <= lax.broadcasted_iota(jnp.int32, (tk, tq), 1))


def _attn_a_kernel(lamp_ref, subc_ref, q1_ref, q2_ref, k_ref, vt_ref, o_ref, m_ref, l_ref, acc_ref,
                   *, lam_init):
    i = pl.program_id(1)
    j = pl.program_id(2)
    tq = q1_ref.shape[0]

    @pl.when(j == 0)
    def _():
        _init_stats(m_ref, l_ref, acc_ref)

    def step(masked):
        keep = _causal_keep(tq, tq) if masked else None
        for h in range(H_A):
            sl = slice(h * LANES, (h + 1) * LANES)
            k = k_ref[:, sl]
            vt = vt_ref[sl, :]
            for comp, q_ref in enumerate((q1_ref, q2_ref)):
                s = _nt(k, q_ref[:, sl])
                if masked:
                    s = jnp.where(keep, s, NEG)
                _online_update(s, vt, m_ref, l_ref, acc_ref, 2 * h + comp)

    @pl.when(j < i)
    def _():
        step(False)

    @pl.when(j == i)
    def _():
        step(True)
        lp = lamp_ref[...]
        lam = (jnp.exp(jnp.sum(lp[0:1] * lp[1:2], axis=-1, keepdims=True))
               - jnp.exp(jnp.sum(lp[2:3] * lp[3:4], axis=-1, keepdims=True)) + lam_init)
        for h in range(H_A):
            o = acc_ref[2 * h] / l_ref[2 * h] - lam * (acc_ref[2 * h + 1] / l_ref[2 * h + 1])
            y = o * lax.rsqrt(jnp.mean(o * o, axis=0, keepdims=True) + EPS) * subc_ref[...]
            o_ref[:, h * LANES:(h + 1) * LANES] = (y * (1.0 - lam_init)).T.astype(o_ref.dtype)


def _attn_a_call(q1, q2, k, vt, lamp, subc, layer, bsz, t, lam_init):
    tq = min(TQ_A, t)
    nq = t // tq
    assert vt.shape[2] == tq
    qspec = pl.BlockSpec((tq, 512), lambda b, i, j: (b * nq + i, 0))
    kspec = pl.BlockSpec((tq, 512), lambda b, i, j: (b * nq + jnp.minimum(i, j), 0))
    vspec = pl.BlockSpec((None, 512, tq), lambda b, i, j: (b * nq + jnp.minimum(i, j), 0, 0))
    return pl.pallas_call(
        functools.partial(_attn_a_kernel, lam_init=lam_init),
        grid=(bsz, nq, nq),
        in_specs=[pl.BlockSpec((None, 8, LANES), lambda b, i, j: (layer, 0, 0)),
                  pl.BlockSpec((None, LANES, 1), lambda b, i, j: (layer, 0, 0)),
                  qspec, qspec, kspec, vspec],
        out_specs=qspec,
        out_shape=jax.ShapeDtypeStruct((bsz * t, 512), BF16),
        scratch_shapes=[pltpu.VMEM((2 * H_A, 1, tq), F32), pltpu.VMEM((2 * H_A, 1, tq), F32),
                        pltpu.VMEM((2 * H_A, DV_A, tq), F32)],
        compiler_params=pltpu.CompilerParams(
            dimension_semantics=("parallel", "parallel", "arbitrary"), vmem_limit_bytes=VMEM_LIMIT),
        name="attn_a",
    )(lamp, subc, q1, q2, k, vt)


def _finish_heads(o_ref, l_ref, acc_ref, n_heads):
    o_t = jnp.concatenate([acc_ref[h] / l_ref[h] for h in range(n_heads)], axis=0)
    o_ref[...] = o_t.T.astype(o_ref.dtype)


def _attn_c_kernel(q_ref, k_ref, vt_ref, ck_ref, cqt_ref, o_ref, m_ref, l_ref, acc_ref):
    i = pl.program_id(1)
    j = pl.program_id(2)
    tq = q_ref.shape[0]

    @pl.when(j == 0)
    def _():
        _init_stats(m_ref, l_ref, acc_ref)

    def step(masked):
        keep = _causal_keep(tq, tq) if masked else None
        ck = ck_ref[...]
        for h in range(H_C):
            pr = slice((h // 2) * LANES, (h // 2 + 1) * LANES)
            s = _nt(k_ref[:, pr], q_ref[:, h * LANES:(h + 1) * LANES])
            s = s + (cqt_ref[h:h + 1, :] - ck[:, h:h + 1])
            if masked:
                s = jnp.where(keep, s, NEG)
            _online_update(s, vt_ref[h * HEAD:(h + 1) * HEAD, :], m_ref, l_ref, acc_ref, h)

    @pl.when(j < i)
    def _():
        step(False)

    @pl.when(j == i)
    def _():
        step(True)
        _finish_heads(o_ref, l_ref, acc_ref, H_C)


def _attn_c_call(qz, k, vt, cc, cct, bsz, t):
    tq = min(TQ_C, t)
    nq = t // tq
    assert vt.shape[2] == tq
    kv = lambda b, i, j: (b * nq + jnp.minimum(i, j), 0)
    return pl.pallas_call(
        _attn_c_kernel,
        grid=(bsz, nq, nq),
        in_specs=[pl.BlockSpec((tq, 512), lambda b, i, j: (b * nq + i, 0)),
                  pl.BlockSpec((tq, 256), kv),
                  pl.BlockSpec((None, 256, tq), lambda b, i, j: (b * nq + jnp.minimum(i, j), 0, 0)),
                  pl.BlockSpec((tq, H_C), kv),
                  pl.BlockSpec((None, 8, tq), lambda b, i, j: (b, 0, i))],
        out_specs=pl.BlockSpec((tq, 256), lambda b, i, j: (b * nq + i, 0)),
        out_shape=jax.ShapeDtypeStruct((bsz * t, 256), BF16),
        scratch_shapes=[pltpu.VMEM((H_C, 1, tq), F32), pltpu.VMEM((H_C, 1, tq), F32),
                        pltpu.VMEM((H_C, D_C, tq), F32)],
        compiler_params=pltpu.CompilerParams(
            dimension_semantics=("parallel", "parallel", "arbitrary"), vmem_limit_bytes=VMEM_LIMIT),
        name="attn_c",
    )(qz, k, vt, cc, cct)


FOLD_ROWS = 64
SWEEP_ROWS = 32
SNAP_AFTER = 24
SNAP_EVERY = 4


def _fold_rows(x, op):
    parts = [x[r:r + FOLD_ROWS] for r in range(0, x.shape[0], FOLD_ROWS)]
    while len(parts) > 1:
        nxt = [op(parts[a], parts[a + 1]) for a in range(0, len(parts) - 1, 2)]
        parts = nxt + (parts[-1:] if len(parts) % 2 else [])
    return parts[0]


def _mixer_b_kernel(iqz_ref, iwt_ref, qz_ref, ik_ref, k_ref, vt_ref, o_ref,
                    slab, lo_s, hi_s, cnt_s, midx_s, m_ref, l_ref, acc_ref, *, top_k, t_seq):
    i = pl.program_id(1)
    tq = qz_ref.shape[0]
    n_chunks = i + 1
    q_pos = i * tq + lax.broadcasted_iota(jnp.int32, (1, tq), 1)
    n_causal = (q_pos + 1).astype(F32)
    kt = jnp.minimum(n_causal, float(top_k))
    iwt = iwt_ref[...]

    def scores(c):
        ik = ik_ref[pl.ds(pl.multiple_of(c * tq, tq), tq), :]
        acc = None
        for h in range(H_IDX):
            s = jnp.maximum(_nt(ik, iqz_ref[:, h * LANES:(h + 1) * LANES]), 0.0) * iwt[h:h + 1, :]
            acc = s if acc is None else acc + s
        return acc

    def fill(c, carry):
        mn, mx = carry
        s = scores(c)
        slab[c] = s
        return jnp.minimum(mn, _fold_rows(s, jnp.minimum)), jnp.maximum(mx, _fold_rows(s, jnp.maximum))

    big = jnp.full((FOLD_ROWS, tq), -NEG, F32)
    mn, mx = lax.fori_loop(0, i, fill, (big, -big))
    s = scores(i)
    keep = _causal_keep(tq, tq)
    slab[i] = jnp.where(keep, s, NEG)
    mn = jnp.minimum(mn, _fold_rows(jnp.where(keep, s, -NEG), jnp.minimum))
    mx = jnp.maximum(mx, _fold_rows(jnp.where(keep, s, NEG), jnp.maximum))
    rowmin = jnp.min(mn, axis=0, keepdims=True)
    rowmax = jnp.max(mx, axis=0, keepdims=True)

    def sweep(term, combine, init):
        def body(c, acc):
            for r in range(0, tq, SWEEP_ROWS):
                acc = combine(acc, term(slab[c, r:r + SWEEP_ROWS, :], c, r))
            return acc
        return lax.fori_loop(0, n_chunks, body, jnp.full((SWEEP_ROWS, tq), init, F32))

    def count(pred):
        acc = sweep(lambda x, c, r: jnp.where(pred(x, c, r), 1.0, 0.0), jnp.add, 0.0)
        return jnp.sum(acc, axis=0, keepdims=True)

    def max_below(bound):
        acc = sweep(lambda x, c, r: jnp.where(x < bound, x, NEG), jnp.maximum, NEG)
        return jnp.max(acc, axis=0, keepdims=True)

    c_max = count(lambda x, c, r: x >= rowmax)
    top_tied = c_max >= kt
    lo_s[...] = jnp.where(top_tied, rowmax, rowmin)
    hi_s[...] = rowmax
    cnt_s[...] = jnp.where(top_tied, c_max, n_causal)

    def probe():
        lo = lo_s[...]
        hi = hi_s[...]
        mid = lo + (hi - lo) * 0.5
        act = (cnt_s[...] != kt) & (mid > lo) & (mid < hi)
        return act, mid

    def n_active():
        act, _ = probe()
        return jnp.max(jnp.where(act, 1.0, 0.0))

    def search(st):
        it, _ = st
        act, mid = probe()
        snap = (it >= SNAP_AFTER) & (((it - SNAP_AFTER) & (SNAP_EVERY - 1)) == 0)

        @pl.when(jnp.logical_not(snap))
        def _():
            c = count(lambda x, ch, r: x >= mid)
            up = act & (c >= kt)
            dn = act & (c < kt)
            lo_s[...] = jnp.where(up, mid, lo_s[...])
            cnt_s[...] = jnp.where(up, c, cnt_s[...])
            hi_s[...] = jnp.where(dn, mid, hi_s[...])

        @pl.when(snap)
        def _():
            v = max_below(hi_s[...])
            c = count(lambda x, ch, r: x >= v)
            hit = act & (c >= kt)
            lo_s[...] = jnp.where(hit, v, lo_s[...])
            cnt_s[...] = jnp.where(hit, c, cnt_s[...])
            hi_s[...] = jnp.where(act, v, hi_s[...])

        return it + 1, n_active()

    lax.while_loop(lambda st: st[1] > 0.5, search, (jnp.int32(0), n_active()))
    thr = lo_s[...]

    midx_s[...] = jnp.full_like(midx_s, float(t_seq))
    key_f = lax.broadcasted_iota(jnp.int32, (tq, 1), 0).astype(F32)

    def key_idx(c, r=0, n=None):
        return key_f[r:r + (n or tq)] + (c * tq).astype(F32)

    @pl.when(jnp.max(jnp.where(cnt_s[...] > kt, 1.0, 0.0)) > 0.5)
    def _():
        need = kt - count(lambda x, c, r: x > thr)
        lo_s[...] = jnp.full_like(lo_s, -1.0)
        hi_s[...] = jnp.full_like(hi_s, float(t_seq - 1))

        def idx_step(_, carry):
            lo_i = lo_s[...]
            hi_i = hi_s[...]
            mid_i = jnp.floor((lo_i + hi_i) * 0.5)
            c = count(lambda x, ch, r: (x == thr) & (key_idx(ch, r, SWEEP_ROWS) <= mid_i))
            ok = c >= need
            hi_s[...] = jnp.where(ok, mid_i, hi_i)
            lo_s[...] = jnp.where(ok, lo_i, mid_i)
            return carry

        lax.fori_loop(0, int(math.ceil(math.log2(t_seq))) + 1, idx_step, 0)
        midx_s[...] = jnp.where(cnt_s[...] > kt, hi_s[...], float(t_seq))

    midx = midx_s[...]

    _init_stats(m_ref, l_ref, acc_ref)

    def attend(c, carry):
        x = slab[c]
        sel = (x > thr) | ((x == thr) & (key_idx(c) <= midx))
        bias = jnp.where(sel, 0.0, NEG)
        off = pl.multiple_of(c * tq, tq)
        for h in range(H_B):
            pr = slice((h // 2) * LANES, (h // 2 + 1) * LANES)
            sc = _nt(k_ref[pl.ds(off, tq), pr], qz_ref[:, h * LANES:(h + 1) * LANES]) + bias
            _online_update(sc, vt_ref[c, h * HEAD:(h + 1) * HEAD, :], m_ref, l_ref, acc_ref, h)
        return carry

    lax.fori_loop(0, n_chunks, attend, 0)
    _finish_heads(o_ref, l_ref, acc_ref, H_B)


def _mixer_b_call(iqz, iwt, qz, ik2, k, vt, bsz, t):
    tq = min(TQ_B, t)
    nq = t // tq
    assert vt.shape[2] == tq
    top_k = min(TOPK_MAX, t // 4)
    qrow = lambda w: pl.BlockSpec((tq, w), lambda b, i: (b * nq + i, 0))
    full = lambda w: pl.BlockSpec((t, w), lambda b, i: (b, 0), pipeline_mode=pl.Buffered(1))
    row = pltpu.VMEM((1, tq), F32)
    return pl.pallas_call(
        functools.partial(_mixer_b_kernel, top_k=top_k, t_seq=t),
        grid=(bsz, nq),
        in_specs=[qrow(1024), pl.BlockSpec((None, H_IDX, tq), lambda b, i: (b, 0, i)), qrow(512),
                  full(LANES), full(256),
                  pl.BlockSpec((nq, 256, tq), lambda b, i: (b, 0, 0), pipeline_mode=pl.Buffered(1))],
        out_specs=qrow(256),
        out_shape=jax.ShapeDtypeStruct((bsz * t, 256), BF16),
        scratch_shapes=[pltpu.VMEM((nq, tq, tq), F32), row, row, row, row,
                        pltpu.VMEM((H_B, 1, tq), F32), pltpu.VMEM((H_B, 1, tq), F32),
                        pltpu.VMEM((H_B, D_B, tq), F32)],
        compiler_params=pltpu.CompilerParams(
            dimension_semantics=("parallel", "arbitrary"), vmem_limit_bytes=VMEM_LIMIT),
        name="mixer_b",
    )(iqz, iwt, qz, ik2, k, vt)


def _merge_kernel(x_ref, g_ref, oa_ref, ob_ref, oc_ref, wg_ref, wa_ref, wb_ref, wc_ref, wo_ref, h_ref):
    x = x_ref[...]
    ms = jnp.mean(x * x, axis=-1, keepdims=True)
    xn = (x * lax.rsqrt(ms + EPS) * g_ref[...]).astype(BF16)
    merged = None
    for br, (o_ref, w_ref) in enumerate(((oa_ref, wa_ref), (ob_ref, wb_ref), (oc_ref, wc_ref))):
        gate = jax.nn.sigmoid(_dot(xn, wg_ref[:, br * D_MODEL:(br + 1) * D_MODEL]))
        term = gate * _dot(o_ref[...], w_ref[...])
        merged = term if merged is None else merged + term
    h_ref[...] = x + _dot(merged.astype(BF16), wo_ref[...])


def _merge_call(x2d, oa, ob, oc, layer, p):
    n = x2d.shape[0]
    tm = min(TM_MERGE, n)
    row = lambda w: pl.BlockSpec((tm, w), lambda i: (i, 0))
    wspec = lambda r, c: pl.BlockSpec((None, r, c), lambda i: (layer, 0, 0), pipeline_mode=pl.Buffered(1))
    return pl.pallas_call(
        _merge_kernel,
        grid=(n // tm,),
        in_specs=[row(D_MODEL), pl.BlockSpec((None, 1, D_MODEL), lambda i: (layer, 0, 0)),
                  row(512), row(256), row(256),
                  wspec(D_MODEL, N_BRANCH * D_MODEL), wspec(512, D_MODEL), wspec(256, D_MODEL),
                  wspec(256, D_MODEL), wspec(D_MODEL, D_MODEL)],
        out_specs=row(D_MODEL),
        out_shape=jax.ShapeDtypeStruct((n, D_MODEL), F32),
        compiler_params=pltpu.CompilerParams(dimension_semantics=("parallel",),
                                             vmem_limit_bytes=VMEM_LIMIT),
        name="merge",
    )(x2d, p["norm_attn"], oa, ob, oc, p["w_g"], p["w_out_a"], p["w_out_b"], p["w_out_c"], p["w_o"])


def _ffn_kernel(h_ref, g_ref, wg_ref, wu_ref, wd_ref, y_ref):
    h = h_ref[...]
    ms = jnp.mean(h * h, axis=-1, keepdims=True)
    hn = (h * lax.rsqrt(ms + EPS) * g_ref[...]).astype(BF16)
    gate = _dot(hn, wg_ref[...])
    up = _dot(hn, wu_ref[...])
    act = (gate * jax.nn.sigmoid(gate) * up).astype(BF16)
    y_ref[...] = h + _dot(act, wd_ref[...])


def _ffn_call(h2d, layer, p):
    n = h2d.shape[0]
    tm = min(TM_FFN, n)
    d_ff = p["w_gate"].shape[-1]
    row = lambda w: pl.BlockSpec((tm, w), lambda i: (i, 0))
    wspec = lambda r, c: pl.BlockSpec((None, r, c), lambda i: (layer, 0, 0), pipeline_mode=pl.Buffered(1))
    return pl.pallas_call(
        _ffn_kernel,
        grid=(n // tm,),
        in_specs=[row(D_MODEL), pl.BlockSpec((None, 1, D_MODEL), lambda i: (layer, 0, 0)),
                  wspec(D_MODEL, d_ff), wspec(D_MODEL, d_ff), wspec(d_ff, D_MODEL)],
        out_specs=row(D_MODEL),
        out_shape=jax.ShapeDtypeStruct((n, D_MODEL), F32),
        compiler_params=pltpu.CompilerParams(dimension_semantics=("parallel",),
                                             vmem_limit_bytes=VMEM_LIMIT),
        name="ffn",
    )(h2d, p["norm_ffn"], p["w_gate"], p["w_up"], p["w_down"])


QROWS = 16
N_CACHE = 8
DECODE_GROUP = 8


def _decode_kernel(pt_ref, lamp_ref, sub_ref,
                   qa_ref, qb_ref, qi_ref, qc_ref, iw_ref,
                   nak_ref, nav_ref, nbk_ref, nbv_ref, nik_ref, nck_ref, ncv_ref, nlf_ref,
                   *refs, lam_init, top_k, n_pages, group):
    cache_refs = [refs[c * group:(c + 1) * group] for c in range(N_CACHE)]
    ak_refs, av_refs, bk_refs, bv_refs, ki_refs, ck_refs, cv_refs, lft_refs = cache_refs
    (tri_ref, oa_ref, ob_ref, oc_ref,
     ma, la, acca, mc, lc, accc, carry, isc_s, sb_s, vb_s) = refs[N_CACHE * group:]
    j = pl.program_id(1)
    n_steps = n_pages // group
    psz = lft_refs[0].shape[1]
    qa = qa_ref[...]
    qb = qb_ref[...]
    qi = qi_ref[...]
    qc = qc_ref[...]
    iw = iw_ref[...]
    lane = lax.broadcasted_iota(jnp.int32, (1, psz), 1)

    def rowdot(q, krow):
        return jnp.sum(q.astype(F32) * krow.astype(F32), axis=-1, keepdims=True)

    @pl.when(j == 0)
    def _():
        ma[...] = rowdot(qa, nak_ref[...])
        la[...] = jnp.ones_like(la)
        acca[...] = nav_ref[...].astype(F32)
        mc[...] = rowdot(qc, nck_ref[...])
        lc[...] = jnp.ones_like(lc)
        accc[...] = jnp.broadcast_to(ncv_ref[...].astype(F32), accc.shape)
        carry[...] = jnp.broadcast_to(nlf_ref[...], carry.shape)
        own_i = jnp.sum(jnp.maximum(rowdot(qi, nik_ref[...]), 0.0) * iw, axis=0, keepdims=True)
        isc_s[...] = jnp.full_like(isc_s, NEG)
        isc_s[pl.ds(n_pages, 1), :] = jnp.where(lane == 0, own_i, NEG)
        sb_s[n_pages] = jnp.broadcast_to(rowdot(qb, nbk_ref[...]), (QROWS, psz))
        first = jnp.where(lane == 0, 1.0, 0.0).astype(F32)
        vb_s[n_pages] = (nbv_ref[...].astype(F32) * first).astype(BF16)

    def online(s, pv, m_ref, l_ref, acc_ref):
        m_prev = m_ref[...]
        m_new = jnp.maximum(m_prev, jnp.max(s, axis=-1, keepdims=True))
        alpha = jnp.exp(m_prev - m_new)
        p = jnp.exp(s - m_new)
        l_ref[...] = alpha * l_ref[...] + jnp.sum(p, axis=-1, keepdims=True)
        acc_ref[...] = alpha * acc_ref[...] + pv(p.astype(BF16))
        m_ref[...] = m_new

    n_rows = ak_refs[0].shape[0]
    key_head = lax.broadcasted_iota(jnp.int32, (QROWS, n_rows), 1) % H_A
    row_head = lax.broadcasted_iota(jnp.int32, (QROWS, n_rows), 0) // 2
    head_bias = jnp.where(key_head == row_head, 0.0, NEG)
    s_a = jnp.concatenate([_nt(qa, r[...].astype(BF16)) + head_bias for r in ak_refs], axis=1)

    def pv_a(p):
        return sum(_dot(p[:, g * n_rows:(g + 1) * n_rows], av_refs[g][...].astype(BF16)) for g in range(group))

    online(s_a, pv_a, ma, la, acca)
    run = carry[...][0:8]
    s_c = []
    for g in range(group):
        lft = jnp.concatenate([lft_refs[g][...], jnp.zeros((8 - H_C, psz), F32)], axis=0)
        suffix = jnp.dot(lft, tri_ref[...], preferred_element_type=F32, precision=lax.Precision.HIGHEST) + run
        bias_c = jnp.concatenate([suffix, jnp.zeros((QROWS - 8, psz), F32)], axis=0)
        s_c.append(_dot(qc, ck_refs[g][...].astype(BF16)) + bias_c)
        run = run + jnp.sum(lft, axis=-1, keepdims=True)
    carry[0:8] = run

    def pv_c(p):
        return sum(_nt(p[:, g * psz:(g + 1) * psz], cv_refs[g][...].astype(BF16)) for g in range(group))

    online(jnp.concatenate(s_c, axis=1), pv_c, mc, lc, accc)
    for g in range(group):
        page = n_pages - 1 - (j * group + g)
        si = jnp.maximum(_dot(qi, ki_refs[g][...].astype(BF16)), 0.0) * iw
        isc_s[pl.ds(page, 1), :] = jnp.sum(si, axis=0, keepdims=True)
        sb_s[page] = _dot(qb, bk_refs[g][...].astype(BF16))
        vb_s[page] = bv_refs[g][...].astype(BF16)

    @pl.when(j == n_steps - 1)
    def _():
        lp = lamp_ref[...]
        lam = (jnp.exp(jnp.sum(lp[0:1] * lp[1:2], axis=-1, keepdims=True))
               - jnp.exp(jnp.sum(lp[2:3] * lp[3:4], axis=-1, keepdims=True)) + lam_init)
        oa = acca[...] / la[...]
        for h in range(H_A):
            sl = slice(h * LANES, (h + 1) * LANES)
            o = oa[2 * h:2 * h + 1, :] - lam * oa[2 * h + 1:2 * h + 2, :]
            y = o * lax.rsqrt(jnp.mean(o * o, axis=-1, keepdims=True) + EPS) * sub_ref[...]
            oa_ref[:, sl] = (y * (1.0 - lam_init)).astype(oa_ref.dtype)
        oc = accc[...] / lc[...]
        for h in range(H_C):
            oc_ref[:, h * HEAD:(h + 1) * HEAD] = oc[h:h + 1, h * HEAD:(h + 1) * HEAD].astype(oc_ref.dtype)

        x = isc_s[...]
        valid = x > NEG * 0.5
        kt = float(top_k)
        idx = (lax.broadcasted_iota(jnp.int32, x.shape, 0) * psz
               + lax.broadcasted_iota(jnp.int32, x.shape, 1)).astype(F32)

        def count(pred):
            return jnp.sum(jnp.where(pred, 1.0, 0.0))

        vmax = jnp.max(x)
        vmin = jnp.min(jnp.where(valid, x, -NEG))
        c_max = count(x >= vmax)
        top_tied = c_max >= kt
        lo0 = jnp.where(top_tied, vmax, vmin)
        cnt0 = jnp.where(top_tied, c_max, count(valid))

        def active(lo, hi, cnt):
            mid = lo + (hi - lo) * 0.5
            return (cnt != kt) & (mid > lo) & (mid < hi)

        def bisect(st):
            lo, hi, cnt = st
            mid = lo + (hi - lo) * 0.5
            c = count(x >= mid)
            up = c >= kt
            return jnp.where(up, mid, lo), jnp.where(up, hi, mid), jnp.where(up, c, cnt)

        thr, _, cnt = lax.while_loop(lambda st: active(*st), bisect, (lo0, vmax, cnt0))
        need = kt - count(x > thr)
        n_keys = float(x.shape[0] * psz)

        def idx_step(_, st):
            lo_i, hi_i = st
            mid_i = jnp.floor((lo_i + hi_i) * 0.5)
            ok = count((x == thr) & (idx <= mid_i)) >= need
            return jnp.where(ok, lo_i, mid_i), jnp.where(ok, mid_i, hi_i)

        _, midx = lax.fori_loop(0, int(math.ceil(math.log2(n_keys))) + 1, idx_step,
                                (jnp.float32(-1.0), jnp.float32(n_keys - 1.0)))
        sel = (x > thr) | ((x == thr) & (idx <= midx))
        bias = jnp.where(sel & valid, 0.0, NEG)
        isc_s[...] = bias

        def pmax(p, m):
            return jnp.maximum(m, jnp.max(sb_s[p] + isc_s[pl.ds(p, 1), :], axis=-1, keepdims=True))

        mb = lax.fori_loop(0, n_pages + 1, pmax, jnp.full((QROWS, 1), NEG, F32), unroll=5)

        def pacc(p, st):
            l, acc = st
            pr = jnp.exp(sb_s[p] + isc_s[pl.ds(p, 1), :] - mb)
            return l + jnp.sum(pr, axis=-1, keepdims=True), acc + _nt(pr.astype(BF16), vb_s[p])

        lb, accb = lax.fori_loop(0, n_pages + 1, pacc,
                                 (jnp.zeros((QROWS, 1), F32), jnp.zeros((QROWS, H_B * D_B), F32)), unroll=5)
        ob = accb / lb
        for h in range(H_B):
            ob_ref[:, h * HEAD:(h + 1) * HEAD] = ob[h:h + 1, h * HEAD:(h + 1) * HEAD].astype(ob_ref.dtype)


def _decode_call(page_table, layer, lam_init, qmats, new_rows, caches, p):
    db, n_pages = page_table.shape
    lft = caches[-1]
    psz = lft.shape[3]
    top_k = min(TOPK_MAX, (n_pages * psz + 1) // 4)
    tri = jnp.triu(jnp.ones((psz, psz), F32), 1).T
    qa, qb, qi, qc, iw = qmats
    group = math.gcd(n_pages, DECODE_GROUP)

    def per_b(a):
        return pl.BlockSpec((None,) + a.shape[1:], lambda b, j, pt: (b,) + (0,) * (a.ndim - 1))

    def paged(a, g):
        return pl.BlockSpec((None, None) + a.shape[2:],
                            lambda b, j, pt: (layer, pt[b, n_pages - 1 - (j * group + g)], 0, 0))

    in_specs = ([pl.BlockSpec((None, 8, LANES), lambda b, j, pt: (layer, 0, 0)),
                 pl.BlockSpec((None, 1, LANES), lambda b, j, pt: (layer, 0, 0))]
                + [per_b(a) for a in (qa, qb, qi, qc, iw)]
                + [per_b(a) for a in new_rows]
                + [paged(a, g) for a in caches for g in range(group)]
                + [pl.BlockSpec((psz, psz), lambda b, j, pt: (0, 0))])
    out = lambda w: pl.BlockSpec((None, 1, w), lambda b, j, pt: (b, 0, 0))
    grid_spec = pltpu.PrefetchScalarGridSpec(
        num_scalar_prefetch=1, grid=(db, n_pages // group), in_specs=in_specs,
        out_specs=(out(512), out(256), out(256)),
        scratch_shapes=[pltpu.VMEM((QROWS, 1), F32), pltpu.VMEM((QROWS, 1), F32), pltpu.VMEM((QROWS, LANES), F32),
                        pltpu.VMEM((QROWS, 1), F32), pltpu.VMEM((QROWS, 1), F32), pltpu.VMEM((QROWS, 256), F32),
                        pltpu.VMEM((QROWS, 1), F32),
                        pltpu.VMEM((-(-(n_pages + 1) // 8) * 8, psz), F32),
                        pltpu.VMEM((n_pages + 1, QROWS, psz), F32),
                        pltpu.VMEM((n_pages + 1, H_B * D_B, psz), BF16)])
    return pl.pallas_call(
        functools.partial(_decode_kernel, lam_init=lam_init, top_k=top_k, n_pages=n_pages, group=group),
        grid_spec=grid_spec,
        out_shape=(jax.ShapeDtypeStruct((db, 1, 512), BF16), jax.ShapeDtypeStruct((db, 1, 256), BF16),
                   jax.ShapeDtypeStruct((db, 1, 256), BF16)),
        compiler_params=pltpu.CompilerParams(dimension_semantics=("parallel", "arbitrary"),
                                             vmem_limit_bytes=VMEM_LIMIT),
        name="decode",
    )(page_table, p["lamp"], p["subln_a"], qa, qb, qi, qc, iw, *new_rows,
      *[a for a in caches for _ in range(group)], tri)


def _rope_tables(pos):
    r = HEAD // ROT_FRAC
    half = r // 2
    inv = jnp.power(ROPE_THETA, -jnp.arange(half, dtype=F32) * 2.0 / r)
    ang = pos.astype(F32)[:, None] * inv[None, :]
    cos, sin = jnp.cos(ang), jnp.sin(ang)
    n = pos.shape[0]
    z = lambda w: jnp.zeros((n, w), F32)
    cos_p = jnp.concatenate([cos, cos, jnp.ones((n, HEAD - r), F32)], axis=1)
    sa_p = jnp.concatenate([-sin, z(HEAD - half)], axis=1)
    sb_p = jnp.concatenate([z(half), sin, z(HEAD - r)], axis=1)
    return tuple(jnp.tile(t, (1, LANES // HEAD)) for t in (cos_p, sa_p, sb_p))


def _prep_params(norm_attn, w_in, b_forget, q_norm_a, k_norm_a, lambda_q1, lambda_k1, lambda_q2, lambda_k2,
                 subln_a, q_norm_b, k_norm_b, k_norm_idx, q_norm_c, k_norm_c, w_out_a, w_out_b, w_out_c,
                 w_o, norm_ffn, w_gate, w_up, w_down):
    depth = w_in.shape[0]
    (aq, ak, av, bq, bk, bv, iq, ik, iw, cq, ck, cv, cf, gl) = jnp.split(
        w_in, np.cumsum(IN_WIDTHS)[:-1].tolist(), axis=-1)
    misc = jnp.concatenate([ik, iw, cf, jnp.zeros((depth, D_MODEL, LANES - D_IDX - H_IDX - H_C), F32)], axis=-1)
    w_mix = jnp.concatenate([aq, ak, av, bq, bk, bv, iq, misc, cq, ck, cv], axis=-1).astype(BF16)

    def row(v, reps, width=512, fill=1.0):
        t = jnp.tile(v, (1, reps))
        return jnp.concatenate([t, jnp.full((depth, width - t.shape[1]), fill, F32)], axis=1)

    misc_gain = jnp.concatenate([k_norm_idx, jnp.ones((depth, 512 - D_IDX), F32)], axis=1)
    bf_row = jnp.concatenate([jnp.zeros((depth, MISC_CF), F32), b_forget,
                              jnp.zeros((depth, 512 - MISC_CF - H_C), F32)], axis=1)
    pvec = jnp.stack([row(q_norm_a, 8), row(k_norm_a, 8), row(q_norm_b, 4), row(k_norm_b, 4),
                      row(q_norm_c, 4), row(k_norm_c, 4), misc_gain, bf_row], axis=1)
    zpad = jnp.zeros((depth, LANES - DK_A), F32)
    lam_rows = [jnp.concatenate([v, zpad], axis=1) for v in (lambda_q1, lambda_k1, lambda_q2, lambda_k2)]
    lamp = jnp.stack(lam_rows + [jnp.zeros((depth, LANES), F32)] * 4, axis=1)
    blk = np.kron(np.eye(512 // HEAD), np.ones((HEAD, HEAD))) / HEAD
    return dict(
        norm_attn=norm_attn[:, None, :], w_mix=w_mix, w_g=gl.astype(BF16), pvec=pvec,
        bd=jnp.asarray(blk, BF16), lamp=lamp, subln_a=subln_a[:, None, :], subln_col=subln_a[:, :, None],
        w_out_a=w_out_a.astype(BF16), w_out_b=w_out_b.astype(BF16), w_out_c=w_out_c.astype(BF16),
        w_o=w_o.astype(BF16), norm_ffn=norm_ffn[:, None, :],
        w_gate=w_gate.astype(BF16), w_up=w_up.astype(BF16), w_down=w_down.astype(BF16))


def _pair_rows(even, odd):
    db, h, w = even.shape
    rows = jnp.stack([even, odd], axis=2).reshape(db, 2 * h, w)
    return jnp.concatenate([rows, jnp.zeros((db, QROWS - 2 * h, w), rows.dtype)], axis=1)


def _decode_qmats(aq1z, aq2z, bqz, iqz, cqz, iw):
    db = aq1z.shape[0]
    qa = _pair_rows(aq1z.reshape(db, H_A, LANES), aq2z.reshape(db, H_A, LANES))

    def unpad(qz, n_heads):
        return qz.reshape(db, n_heads // 2, 2, LANES).sum(axis=2).reshape(db, n_heads * HEAD)

    def head_rows(q, n_heads):
        m = np.zeros((QROWS, n_heads * HEAD), np.float32)
        for h in range(n_heads):
            m[h, h * HEAD:(h + 1) * HEAD] = 1.0
        return q[:, None, :] * jnp.asarray(m, BF16)

    qb = head_rows(unpad(bqz, H_B), H_B)
    qc = head_rows(unpad(cqz, H_C), H_C)
    iq = unpad(iqz, H_IDX).reshape(db, H_IDX, HEAD)
    qi = jnp.concatenate([iq, jnp.zeros((db, QROWS - H_IDX, HEAD), BF16)], axis=1)
    iwc = jnp.concatenate([iw, jnp.zeros((db, QROWS - H_IDX), F32)], axis=1)[:, :, None]
    return qa, qb, qi, qc, iwc


def kernel(x_prompt, x_sample, cache_a_k, cache_a_v, cache_b_k, cache_b_v, cache_b_kidx, cache_c_k, cache_c_v, cache_c_logf, page_table, norm_attn, w_in, b_forget, q_norm_a, k_norm_a, lambda_q1, lambda_k1, lambda_q2, lambda_k2, subln_a, q_norm_b, k_norm_b, k_norm_idx, q_norm_c, k_norm_c, w_out_a, w_out_b, w_out_c, w_o, norm_ffn, w_gate, w_up, w_down):
    bsz, t, _ = x_prompt.shape
    db, ds, _ = x_sample.shape
    assert ds == 1
    depth = w_in.shape[0]
    n_pool, psz = cache_a_k.shape[1], cache_a_k.shape[2]
    past_len = page_table.shape[1] * psz
    p = _prep_params(norm_attn, w_in, b_forget, q_norm_a, k_norm_a, lambda_q1, lambda_k1, lambda_q2,
                     lambda_k2, subln_a, q_norm_b, k_norm_b, k_norm_idx, q_norm_c, k_norm_c,
                     w_out_a, w_out_b, w_out_c, w_o, norm_ffn, w_gate, w_up, w_down)
    tabs_p = _rope_tables(jnp.arange(t, dtype=jnp.int32))
    tabs_s = _rope_tables(jnp.full((db,), past_len, jnp.int32))
    rows_view = lambda c: c.reshape(depth, n_pool, psz * c.shape[3], c.shape[4])
    tr_view = lambda c: jnp.transpose(c, (0, 1, 3, 4, 2)).reshape(depth, n_pool, c.shape[3] * c.shape[4], psz)
    caches = (rows_view(cache_a_k), rows_view(cache_a_v), tr_view(cache_b_k), tr_view(cache_b_v),
              jnp.swapaxes(cache_b_kidx, 2, 3), tr_view(cache_c_k), tr_view(cache_c_v),
              jnp.swapaxes(cache_c_logf, 2, 3))

    xp = x_prompt.reshape(bsz * t, D_MODEL)
    xs = x_sample.reshape(db, D_MODEL)
    rows_p, rows_s = [], []
    for l in range(depth):
        lam_init = 0.8 - 0.6 * math.exp(-0.3 * l)
        (ak, av, bk, bv, ik, ck, cv, lf, aq1, aq2, akb, avb, bqz, bkb, bvb, iqz, ik2, iw, cqz, ckb, cvb,
         cc) = _proj_call(xp, l, p, tabs_p, t, True)
        rows_p.append((ak, av, bk, bv, ik, ck, cv, lf))
        oa = _attn_a_call(aq1, aq2, akb, avb, p["lamp"], p["subln_col"], l, bsz, t, lam_init)
        iwt = jnp.swapaxes(iw.reshape(bsz, t, H_IDX), 1, 2)
        ob = _mixer_b_call(iqz, iwt, bqz, ik2, bkb, bvb, bsz, t)
        cct = jnp.swapaxes(cc.reshape(bsz, t, H_C), 1, 2)
        cct = jnp.concatenate([cct, jnp.zeros((bsz, 8 - H_C, t), F32)], axis=1)
        oc = _attn_c_call(cqz, ckb, cvb, cc, cct, bsz, t)
        xp = _ffn_call(_merge_call(xp, oa, ob, oc, l, p), l, p)
        (ak, av, bk, bv, ik, ck, cv, lf, aq1, aq2, akb, avb, bqz, bkb, bvb, iqz, ik2, iw, cqz, ckb, cvb,
         _) = _proj_call(xs, l, p, tabs_s, db, False)
        rows_s.append((ak, av, bk, bv, ik, ck, cv, lf))
        qmats = _decode_qmats(aq1, aq2, bqz, iqz, cqz, iw)
        lf8 = jnp.concatenate([lf, jnp.zeros((db, QROWS - H_C), F32)], axis=1)[:, :, None]
        ak_h = akb.reshape(db, H_A, LANES)
        av_h = avb.reshape(db, H_A, LANES)
        new_rows = (_pair_rows(ak_h, ak_h), _pair_rows(av_h, av_h), bkb[:, None, :],
                    bvb.astype(F32)[:, :, None], ik2[:, None, 0:D_IDX], ckb[:, None, :], cvb[:, None, :], lf8)
        oa, ob, oc = _decode_call(page_table, l, lam_init, qmats, new_rows, caches, p)
        xs = _ffn_call(_merge_call(xs, oa.reshape(db, 512), ob.reshape(db, 256), oc.reshape(db, 256), l, p),
                       l, p)

    def stack(rows, idx, lead, tail):
        return jnp.stack([r[idx] for r in rows], axis=0).reshape((depth,) + lead + tail)

    tails = ((H_A, 2 * DK_A), (H_A, DV_A), (H_B, D_B), (H_B, D_B), (D_IDX,), (H_C, D_C), (H_C, D_C), (H_C,))
    outs_p = [stack(rows_p, i, (bsz, t), tails[i]) for i in range(8)]
    outs_s = [stack(rows_s, i, (db, 1), tails[i]) for i in range(8)]
    return (xp.reshape(bsz, t, D_MODEL), xs.reshape(db, 1, D_MODEL), *outs_p, *outs_s)
```

```python
import functools
import math

import numpy as np
import jax
import jax.numpy as jnp
from jax import lax
from jax.experimental import pallas as pl
from jax.experimental.pallas import tpu as pltpu

F32 = jnp.float32
BF16 = jnp.bfloat16

D_MODEL = 1024
H_A, DK_A, DV_A = 4, 64, 128
H_B, D_B = 4, 64
H_IDX, D_IDX = 8, 64
TOPK_MAX = 256
H_C, D_C = 4, 64
N_BRANCH = 3
ROPE_THETA = 500000.0
ROT_FRAC = 4
EPS = 1e-6
NEG = -1e30
IN_WIDTHS = (H_A * 2 * DK_A, H_A * 2 * DK_A, H_A * DV_A,
             H_B * D_B, H_B * D_B, H_B * D_B,
             H_IDX * D_IDX, D_IDX, H_IDX,
             H_C * D_C, H_C * D_C, H_C * D_C, H_C,
             N_BRANCH * D_MODEL)

LANES = 128
ONES_ROWS = 16
HEAD = 64
QK_SCALE = HEAD ** -0.5

OFF_AQ, OFF_AK, OFF_AV = 0, 512, 1024
OFF_BQ, OFF_BK, OFF_BV = 1536, 1792, 2048
OFF_IQ, OFF_MISC = 2304, 2816
OFF_CQ, OFF_CK, OFF_CV = 2944, 3200, 3456
W_MIX = 3712
MISC_IW, MISC_CF = 64, 72

TM_PROJ = 512
TQ_A = 512
TQ_C = 512
TQ_B = 512
TM_MERGE = 512
TM_FFN = 512
VMEM_LIMIT = 56 * 1024 * 1024


def _nt(a, b):
    return lax.dot_general(a, b, (((1,), (1,)), ((), ())), preferred_element_type=F32)


def _dot(a, b):
    return jnp.dot(a, b, preferred_element_type=F32)


def _const_spec(shape, index):
    n = len(index)
    return pl.BlockSpec(shape, lambda *_: index, pipeline_mode=pl.Buffered(1))


def _proj_kernel(x_ref, g_ref, w_ref, pv_ref, bd_ref, cos_ref, sa_ref, sb_ref, tri_ref,
                 ak_o, av_o, bk_o, bv_o, ik_o, ck_o, cv_o, lf_o,
                 aq1_o, aq2_o, akb_o, avb_o, bqz_o, bkb_o, bvb_o, iqz_o, ik2_o, iw_o,
                 cqz_o, ckb_o, cvb_o, cc_o, carry_ref, *, tiles_per_seq, transposed_v):
    i = pl.program_id(0)
    x = x_ref[...]
    ms = jnp.mean(x * x, axis=-1, keepdims=True)
    xn = (x * lax.rsqrt(ms + EPS) * g_ref[...]).astype(BF16)
    cos = cos_ref[...]
    sa = sa_ref[...]
    sb = sb_ref[...]
    lane = lax.broadcasted_iota(jnp.int32, (1, LANES), 1)
    lo = jnp.where(lane < HEAD, 1.0, 0.0).astype(F32)
    hi = 1.0 - lo

    def mm(off, w):
        return _dot(xn, w_ref[:, off:off + w])

    def tile(t, w):
        return t if w == LANES else jnp.concatenate([t] * (w // LANES), axis=1)

    def norm(p, row, w):
        x2 = p * p
        h = x2.astype(BF16)
        l = (x2 - h.astype(F32)).astype(BF16)
        bd = bd_ref[0:w, 0:w]
        m = _dot(h, bd) + _dot(l, bd)
        return p * lax.rsqrt(m + EPS) * pv_ref[row:row + 1, 0:w]

    def rope(y, w):
        return (y * tile(cos, w) + pltpu.roll(y, w - 8, 1) * tile(sa, w)
                + pltpu.roll(y, 8, 1) * tile(sb, w))

    def vcast(v, n_heads):
        if not transposed_v:
            return v.astype(BF16)
        vt = v.T
        d = vt.shape[0] // n_heads
        ones = jnp.ones((ONES_ROWS, vt.shape[1]), F32)
        parts = []
        for h in range(n_heads):
            parts += [vt[h * d:(h + 1) * d], ones]
        return jnp.concatenate(parts, axis=0).astype(BF16)

    def pad_heads(y, n_pairs):
        parts = []
        for j in range(n_pairs):
            col = y[:, j * LANES:(j + 1) * LANES]
            parts.append(col * lo)
            parts.append(col * hi)
        return jnp.concatenate(parts, axis=1)

    aq = rope(norm(mm(OFF_AQ, 512), 0, 512), 512) * QK_SCALE
    aq1_o[...] = (aq * tile(lo, 512)).astype(BF16)
    aq2_o[...] = (aq * tile(hi, 512)).astype(BF16)
    ak = rope(norm(mm(OFF_AK, 512), 1, 512), 512)
    ak_o[...] = ak
    akb_o[...] = ak.astype(BF16)
    av = mm(OFF_AV, 512)
    av_o[...] = av
    avb_o[...] = vcast(av, H_A)
    bq = rope(norm(mm(OFF_BQ, 256), 2, 256), 256) * QK_SCALE
    bqz_o[...] = pad_heads(bq, 2).astype(BF16)
    bk = rope(norm(mm(OFF_BK, 256), 3, 256), 256)
    bk_o[...] = bk
    bkb_o[...] = bk.astype(BF16)
    bv = mm(OFF_BV, 256)
    bv_o[...] = bv
    bvb_o[...] = vcast(bv, H_B)
    iq = rope(mm(OFF_IQ, 512), 512) * QK_SCALE
    iqz_o[...] = pad_heads(iq, 4).astype(BF16)
    misc = mm(OFF_MISC, LANES)
    ik = rope(norm(misc, 6, LANES), LANES)
    ik_o[...] = ik[:, 0:D_IDX]
    ik2_o[...] = jnp.where(lane < HEAD, ik, pltpu.roll(ik, HEAD, 1)).astype(BF16)
    iw_o[...] = misc[:, MISC_IW:MISC_IW + H_IDX] * (H_IDX ** -0.5)
    cq = norm(mm(OFF_CQ, 256), 4, 256) * QK_SCALE
    cqz_o[...] = pad_heads(cq, 2).astype(BF16)
    ck = norm(mm(OFF_CK, 256), 5, 256)
    ck_o[...] = ck
    ckb_o[...] = ck.astype(BF16)
    cv = mm(OFF_CV, 256)
    cv_o[...] = cv
    cvb_o[...] = vcast(cv, H_C)
    z = misc + pv_ref[7:8, 0:LANES]
    lf = jnp.minimum(z, 0.0) - jnp.log1p(jnp.exp(-jnp.abs(z)))
    lf_o[...] = lf[:, MISC_CF:MISC_CF + H_C]

    @pl.when(i % tiles_per_seq == 0)
    def _():
        carry_ref[...] = jnp.zeros_like(carry_ref)

    cum = jnp.dot(tri_ref[...], lf, preferred_element_type=F32,
                  precision=lax.Precision.HIGHEST) + carry_ref[...]
    cc_o[...] = cum[:, MISC_CF:MISC_CF + H_C]
    tm = cum.shape[0]
    carry_ref[...] = cum[tm - 1:tm, :]


def _proj_call(x2d, layer, p, tabs, t_seq, transposed_v):
    n = x2d.shape[0]
    tm = min(TM_PROJ, n)
    assert n % tm == 0 and t_seq % tm == 0
    cos, sa, sb = tabs
    ntab = cos.shape[0] // tm
    tri = jnp.tril(jnp.ones((tm, tm), F32))
    row = lambda w: pl.BlockSpec((tm, w), lambda i: (i, 0))
    tab = pl.BlockSpec((tm, LANES), lambda i: (i % ntab, 0))
    widths_f32 = (512, 512, 256, 256, D_IDX, 256, 256, H_C)
    outs_bf = ((512, BF16), (512, BF16), (512, BF16), (512, BF16), (512, BF16), (256, BF16), (256, BF16),
               (1024, BF16), (LANES, BF16), (H_IDX, F32), (512, BF16), (256, BF16), (256, BF16), (H_C, F32))
    v_slots = (3, 6, 12) if transposed_v else ()
    vt_rows = lambda w: w + (w // (DV_A if w == H_A * DV_A else HEAD)) * ONES_ROWS
    out_shape = tuple(jax.ShapeDtypeStruct((n, w), F32) for w in widths_f32) + tuple(
        jax.ShapeDtypeStruct((n // tm, vt_rows(w), tm) if s in v_slots else (n, w), d)
        for s, (w, d) in enumerate(outs_bf))
    out_specs = tuple(row(w) for w in widths_f32) + tuple(
        pl.BlockSpec((None, vt_rows(w), tm), lambda i: (i, 0, 0)) if s in v_slots else row(w)
        for s, (w, _) in enumerate(outs_bf))
    return pl.pallas_call(
        functools.partial(_proj_kernel, tiles_per_seq=t_seq // tm, transposed_v=transposed_v),
        grid=(n // tm,),
        in_specs=[
            row(D_MODEL),
            pl.BlockSpec((None, 1, D_MODEL), lambda i: (layer, 0, 0)),
            pl.BlockSpec((None, D_MODEL, W_MIX), lambda i: (layer, 0, 0), pipeline_mode=pl.Buffered(1)),
            pl.BlockSpec((None, 8, 512), lambda i: (layer, 0, 0)),
            _const_spec((512, 512), (0, 0)),
            tab, tab, tab,
            _const_spec((tm, tm), (0, 0)),
        ],
        out_specs=out_specs,
        out_shape=out_shape,
        scratch_shapes=[pltpu.VMEM((1, LANES), F32)],
        compiler_params=pltpu.CompilerParams(dimension_semantics=("arbitrary",),
                                             vmem_limit_bytes=VMEM_LIMIT),
        name="proj",
    )(x2d, p["norm_attn"], p["w_mix"], p["pvec"], p["bd"], cos, sa, sb, tri)


def _online_update(s_t, v_t, m_ref, acc_ref, idx):
    m_prev = m_ref[idx]
    m_new = jnp.maximum(m_prev, jnp.max(s_t, axis=0, keepdims=True))
    alpha = jnp.exp(m_prev - m_new)
    p = jnp.exp(s_t - m_new).astype(BF16)
    acc_ref[idx] = alpha * acc_ref[idx] + _dot(v_t, p)
    m_ref[idx] = m_new


def _normalized(acc, d):
    return acc[0:d] / acc[d:d + 1]


def _init_stats(m_ref, acc_ref):
    m_ref[...] = jnp.full_like(m_ref, NEG)
    acc_ref[...] = jnp.zeros_like(acc_ref)


def _causal_pairs(nq):
    qi = np.concatenate([np.full(i + 1, i, np.int32) for i in range(nq)])
    kj = np.concatenate([np.arange(i + 1, dtype=np.int32) for i in range(nq)])
    return jnp.asarray(qi), jnp.asarray(kj)


SCORES_AHEAD = 3


def _pipelined_tiles(n_tiles, scores, consume):
    pending = [scores(t) for t in range(min(SCORES_AHEAD, n_tiles))]
    for t in range(n_tiles):
        s = pending.pop(0)
        if t + SCORES_AHEAD < n_tiles:
            pending.append(scores(t + SCORES_AHEAD))
        consume(t, s)


def _causal_keep(tk, tq):
    return (lax.broadcasted_iota(jnp.int32, (tk, tq), 0) <= lax.broadcasted_iota(jnp.int32, (tk, tq), 1))


def _attn_a_kernel(qi_ref, kj_ref, lamp_ref, subc_ref, q1_ref, q2_ref, k_ref, vt_ref, o_ref, m_ref, acc_ref,
                   *, lam_init):
    i = qi_ref[pl.program_id(1)]
    j = kj_ref[pl.program_id(1)]
    tq = q1_ref.shape[0]
    dv = DV_A + ONES_ROWS

    @pl.when(j == 0)
    def _():
        _init_stats(m_ref, acc_ref)

    def step(masked):
        keep = _causal_keep(tq, tq) if masked else None

        def scores(t):
            sl = slice((t // 2) * LANES, (t // 2 + 1) * LANES)
            s = _nt(k_ref[:, sl], (q1_ref, q2_ref)[t % 2][:, sl])
            return jnp.where(keep, s, NEG) if masked else s

        _pipelined_tiles(2 * H_A, scores, lambda t, s: _online_update(
            s, vt_ref[(t // 2) * dv:(t // 2 + 1) * dv, :], m_ref, acc_ref, t))

    @pl.when(j < i)
    def _():
        step(False)

    @pl.when(j == i)
    def _():
        step(True)
        lp = lamp_ref[...]
        lam = (jnp.exp(jnp.sum(lp[0:1] * lp[1:2], axis=-1, keepdims=True))
               - jnp.exp(jnp.sum(lp[2:3] * lp[3:4], axis=-1, keepdims=True)) + lam_init)
        for h in range(H_A):
            o = _normalized(acc_ref[2 * h], DV_A) - lam * _normalized(acc_ref[2 * h + 1], DV_A)
            y = o * lax.rsqrt(jnp.mean(o * o, axis=0, keepdims=True) + EPS) * subc_ref[...]
            o_ref[:, h * LANES:(h + 1) * LANES] = (y * (1.0 - lam_init)).T.astype(o_ref.dtype)


def _attn_a_call(q1, q2, k, vt, lamp, subc, layer, bsz, t, lam_init):
    tq = min(TQ_A, t)
    nq = t // tq
    assert vt.shape[2] == tq
    qi, kj = _causal_pairs(nq)
    qspec = pl.BlockSpec((tq, 512), lambda b, s, qi, kj: (b * nq + qi[s], 0))
    kspec = pl.BlockSpec((tq, 512), lambda b, s, qi, kj: (b * nq + kj[s], 0))
    vspec = pl.BlockSpec((None, vt.shape[1], tq), lambda b, s, qi, kj: (b * nq + kj[s], 0, 0))
    grid_spec = pltpu.PrefetchScalarGridSpec(
        num_scalar_prefetch=2, grid=(bsz, qi.shape[0]),
        in_specs=[pl.BlockSpec((None, 8, LANES), lambda b, s, qi, kj: (layer, 0, 0)),
                  pl.BlockSpec((None, LANES, 1), lambda b, s, qi, kj: (layer, 0, 0)),
                  qspec, qspec, kspec, vspec],
        out_specs=qspec,
        scratch_shapes=[pltpu.VMEM((2 * H_A, 1, tq), F32),
                        pltpu.VMEM((2 * H_A, DV_A + ONES_ROWS, tq), F32)])
    return pl.pallas_call(
        functools.partial(_attn_a_kernel, lam_init=lam_init),
        grid_spec=grid_spec,
        out_shape=jax.ShapeDtypeStruct((bsz * t, 512), BF16),
        compiler_params=pltpu.CompilerParams(
            dimension_semantics=("parallel", "arbitrary"), vmem_limit_bytes=VMEM_LIMIT),
        name="attn_a",
    )(qi, kj, lamp, subc, q1, q2, k, vt)


HEAD_ROWS = HEAD + ONES_ROWS


def _finish_heads(o_ref, acc_ref, n_heads):
    o_t = jnp.concatenate([_normalized(acc_ref[h], HEAD) for h in range(n_heads)], axis=0)
    o_ref[...] = o_t.T.astype(o_ref.dtype)


def _attn_c_kernel(qi_ref, kj_ref, q_ref, k_ref, vt_ref, ck_ref, cqt_ref, o_ref, m_ref, acc_ref):
    i = qi_ref[pl.program_id(1)]
    j = kj_ref[pl.program_id(1)]
    tq = q_ref.shape[0]

    @pl.when(j == 0)
    def _():
        _init_stats(m_ref, acc_ref)

    def step(masked):
        keep = _causal_keep(tq, tq) if masked else None
        ck = ck_ref[...]

        def scores(h):
            pr = slice((h // 2) * LANES, (h // 2 + 1) * LANES)
            s = _nt(k_ref[:, pr], q_ref[:, h * LANES:(h + 1) * LANES])
            s = s + (cqt_ref[h:h + 1, :] - ck[:, h:h + 1])
            return jnp.where(keep, s, NEG) if masked else s

        _pipelined_tiles(H_C, scores, lambda h, s: _online_update(
            s, vt_ref[h * HEAD_ROWS:(h + 1) * HEAD_ROWS, :], m_ref, acc_ref, h))

    @pl.when(j < i)
    def _():
        step(False)

    @pl.when(j == i)
    def _():
        step(True)
        _finish_heads(o_ref, acc_ref, H_C)


def _attn_c_call(qz, k, vt, cc, cct, bsz, t):
    tq = min(TQ_C, t)
    nq = t // tq
    assert vt.shape[2] == tq
    qi, kj = _causal_pairs(nq)
    qrow = lambda b, s, qi, kj: (b * nq + qi[s], 0)
    kv = lambda b, s, qi, kj: (b * nq + kj[s], 0)
    grid_spec = pltpu.PrefetchScalarGridSpec(
        num_scalar_prefetch=2, grid=(bsz, qi.shape[0]),
        in_specs=[pl.BlockSpec((tq, 512), qrow),
                  pl.BlockSpec((tq, 256), kv),
                  pl.BlockSpec((None, vt.shape[1], tq), lambda b, s, qi, kj: (b * nq + kj[s], 0, 0)),
                  pl.BlockSpec((tq, H_C), kv),
                  pl.BlockSpec((None, 8, tq), lambda b, s, qi, kj: (b, 0, qi[s]))],
        out_specs=pl.BlockSpec((tq, 256), qrow),
        scratch_shapes=[pltpu.VMEM((H_C, 1, tq), F32), pltpu.VMEM((H_C, HEAD_ROWS, tq), F32)])
    return pl.pallas_call(
        _attn_c_kernel,
        grid_spec=grid_spec,
        out_shape=jax.ShapeDtypeStruct((bsz * t, 256), BF16),
        compiler_params=pltpu.CompilerParams(
            dimension_semantics=("parallel", "arbitrary"), vmem_limit_bytes=VMEM_LIMIT),
        name="attn_c",
    )(qi, kj, qz, k, vt, cc, cct)


FOLD_ROWS = 64
SWEEP_ROWS = 32
SNAP_AFTER = 22
SNAP_EVERY = 4


def _fold_rows(x, op):
    parts = [x[r:r + FOLD_ROWS] for r in range(0, x.shape[0], FOLD_ROWS)]
    while len(parts) > 1:
        nxt = [op(parts[a], parts[a + 1]) for a in range(0, len(parts) - 1, 2)]
        parts = nxt + (parts[-1:] if len(parts) % 2 else [])
    return parts[0]


def _mixer_b_kernel(iqz_ref, iwt_ref, qz_ref, ik_ref, k_ref, vt_ref, o_ref,
                    slab, lo_s, hi_s, cnt_s, midx_s, m_ref, acc_ref, *, top_k, t_seq):
    i = pl.program_id(1)
    tq = qz_ref.shape[0]
    n_chunks = i + 1
    q_pos = i * tq + lax.broadcasted_iota(jnp.int32, (1, tq), 1)
    n_causal = (q_pos + 1).astype(F32)
    kt = jnp.minimum(n_causal, float(top_k))
    iwt = iwt_ref[...]

    def scores(c):
        ik = ik_ref[pl.ds(pl.multiple_of(c * tq, tq), tq), :]
        acc = None
        for h in range(H_IDX):
            s = jnp.maximum(_nt(ik, iqz_ref[:, h * LANES:(h + 1) * LANES]), 0.0) * iwt[h:h + 1, :]
            acc = s if acc is None else acc + s
        return acc

    def fill(c, carry):
        mn, mx = carry
        s = scores(c)
        slab[c] = s
        return jnp.minimum(mn, _fold_rows(s, jnp.minimum)), jnp.maximum(mx, _fold_rows(s, jnp.maximum))

    big = jnp.full((FOLD_ROWS, tq), -NEG, F32)
    mn, mx = lax.fori_loop(0, i, fill, (big, -big))
    s = scores(i)
    keep = _causal_keep(tq, tq)
    slab[i] = jnp.where(keep, s, NEG)
    mn = jnp.minimum(mn, _fold_rows(jnp.where(keep, s, -NEG), jnp.minimum))
    mx = jnp.maximum(mx, _fold_rows(jnp.where(keep, s, NEG), jnp.maximum))
    rowmin = jnp.min(mn, axis=0, keepdims=True)
    rowmax = jnp.max(mx, axis=0, keepdims=True)

    def sweep(term, combine, init):
        def body(c, acc):
            for r in range(0, tq, SWEEP_ROWS):
                acc = combine(acc, term(slab[c, r:r + SWEEP_ROWS, :], c, r))
            return acc
        return lax.fori_loop(0, n_chunks, body, jnp.full((SWEEP_ROWS, tq), init, F32))

    def count(pred):
        acc = sweep(lambda x, c, r: jnp.where(pred(x, c, r), 1.0, 0.0), jnp.add, 0.0)
        return jnp.sum(acc, axis=0, keepdims=True)

    def max_below(bound):
        acc = sweep(lambda x, c, r: jnp.where(x < bound, x, NEG), jnp.maximum, NEG)
        return jnp.max(acc, axis=0, keepdims=True)

    c_max = count(lambda x, c, r: x >= rowmax)
    top_tied = c_max >= kt
    lo_s[...] = jnp.where(top_tied, rowmax, rowmin)
    hi_s[...] = rowmax
    cnt_s[...] = jnp.where(top_tied, c_max, n_causal)

    def probe():
        lo = lo_s[...]
        hi = hi_s[...]
        mid = lo + (hi - lo) * 0.5
        act = (cnt_s[...] != kt) & (mid > lo) & (mid < hi)
        return act, mid

    def n_active():
        act, _ = probe()
        return jnp.max(jnp.where(act, 1.0, 0.0))

    def search(st):
        it, _ = st
        act, mid = probe()
        snap = (it >= SNAP_AFTER) & (((it - SNAP_AFTER) & (SNAP_EVERY - 1)) == 0)

        @pl.when(jnp.logical_not(snap))
        def _():
            c = count(lambda x, ch, r: x >= mid)
            up = act & (c >= kt)
            dn = act & (c < kt)
            lo_s[...] = jnp.where(up, mid, lo_s[...])
            cnt_s[...] = jnp.where(up, c, cnt_s[...])
            hi_s[...] = jnp.where(dn, mid, hi_s[...])

        @pl.when(snap)
        def _():
            v = max_below(hi_s[...])
            c = count(lambda x, ch, r: x >= v)
            hit = act & (c >= kt)
            lo_s[...] = jnp.where(hit, v, lo_s[...])
            cnt_s[...] = jnp.where(hit, c, cnt_s[...])
            hi_s[...] = jnp.where(act, v, hi_s[...])

        return it + 1, n_active()

    lax.while_loop(lambda st: st[1] > 0.5, search, (jnp.int32(0), n_active()))
    thr = lo_s[...]

    midx_s[...] = jnp.full_like(midx_s, float(t_seq))
    key_f = lax.broadcasted_iota(jnp.int32, (tq, 1), 0).astype(F32)

    def key_idx(c, r=0, n=None):
        return key_f[r:r + (n or tq)] + (c * tq).astype(F32)

    @pl.when(jnp.max(jnp.where(cnt_s[...] > kt, 1.0, 0.0)) > 0.5)
    def _():
        need = kt - count(lambda x, c, r: x > thr)
        lo_s[...] = jnp.full_like(lo_s, -1.0)
        hi_s[...] = jnp.full_like(hi_s, float(t_seq - 1))

        def idx_step(_, carry):
            lo_i = lo_s[...]
            hi_i = hi_s[...]
            mid_i = jnp.floor((lo_i + hi_i) * 0.5)
            c = count(lambda x, ch, r: (x == thr) & (key_idx(ch, r, SWEEP_ROWS) <= mid_i))
            ok = c >= need
            hi_s[...] = jnp.where(ok, mid_i, hi_i)
            lo_s[...] = jnp.where(ok, lo_i, mid_i)
            return carry

        lax.fori_loop(0, int(math.ceil(math.log2(t_seq))) + 1, idx_step, 0)
        midx_s[...] = jnp.where(cnt_s[...] > kt, hi_s[...], float(t_seq))

    midx = midx_s[...]

    _init_stats(m_ref, acc_ref)

    def attend(c, carry):
        x = slab[c]
        sel = (x > thr) | ((x == thr) & (key_idx(c) <= midx))
        bias = jnp.where(sel, 0.0, NEG)
        off = pl.multiple_of(c * tq, tq)

        def scores(h):
            pr = slice((h // 2) * LANES, (h // 2 + 1) * LANES)
            return _nt(k_ref[pl.ds(off, tq), pr], qz_ref[:, h * LANES:(h + 1) * LANES]) + bias

        _pipelined_tiles(H_B, scores, lambda h, s: _online_update(
            s, vt_ref[c, h * HEAD_ROWS:(h + 1) * HEAD_ROWS, :], m_ref, acc_ref, h))
        return carry

    lax.fori_loop(0, n_chunks, attend, 0)
    _finish_heads(o_ref, acc_ref, H_B)


def _mixer_b_call(iqz, iwt, qz, ik2, k, vt, bsz, t):
    tq = min(TQ_B, t)
    nq = t // tq
    assert vt.shape[2] == tq
    top_k = min(TOPK_MAX, t // 4)
    qrow = lambda w: pl.BlockSpec((tq, w), lambda b, i: (b * nq + i, 0))
    full = lambda w: pl.BlockSpec((t, w), lambda b, i: (b, 0), pipeline_mode=pl.Buffered(1))
    row = pltpu.VMEM((1, tq), F32)
    return pl.pallas_call(
        functools.partial(_mixer_b_kernel, top_k=top_k, t_seq=t),
        grid=(bsz, nq),
        in_specs=[qrow(1024), pl.BlockSpec((None, H_IDX, tq), lambda b, i: (b, 0, i)), qrow(512),
                  full(LANES), full(256),
                  pl.BlockSpec((nq, vt.shape[1], tq), lambda b, i: (b, 0, 0), pipeline_mode=pl.Buffered(1))],
        out_specs=qrow(256),
        out_shape=jax.ShapeDtypeStruct((bsz * t, 256), BF16),
        scratch_shapes=[pltpu.VMEM((nq, tq, tq), F32), row, row, row, row,
                        pltpu.VMEM((H_B, 1, tq), F32), pltpu.VMEM((H_B, HEAD_ROWS, tq), F32)],
        compiler_params=pltpu.CompilerParams(
            dimension_semantics=("parallel", "arbitrary"), vmem_limit_bytes=VMEM_LIMIT),
        name="mixer_b",
    )(iqz, iwt, qz, ik2, k, vt)


def _merge_kernel(x_ref, g_ref, oa_ref, ob_ref, oc_ref, wg_ref, wa_ref, wb_ref, wc_ref, wo_ref, h_ref):
    x = x_ref[...]
    ms = jnp.mean(x * x, axis=-1, keepdims=True)
    xn = (x * lax.rsqrt(ms + EPS) * g_ref[...]).astype(BF16)
    merged = None
    for br, (o_ref, w_ref) in enumerate(((oa_ref, wa_ref), (ob_ref, wb_ref), (oc_ref, wc_ref))):
        gate = jax.nn.sigmoid(_dot(xn, wg_ref[:, br * D_MODEL:(br + 1) * D_MODEL]))
        term = gate * _dot(o_ref[...], w_ref[...])
        merged = term if merged is None else merged + term
    h_ref[...] = x + _dot(merged.astype(BF16), wo_ref[...])


def _merge_call(x2d, oa, ob, oc, layer, p):
    n = x2d.shape[0]
    tm = min(TM_MERGE, n)
    row = lambda w: pl.BlockSpec((tm, w), lambda i: (i, 0))
    wspec = lambda r, c: pl.BlockSpec((None, r, c), lambda i: (layer, 0, 0), pipeline_mode=pl.Buffered(1))
    return pl.pallas_call(
        _merge_kernel,
        grid=(n // tm,),
        in_specs=[row(D_MODEL), pl.BlockSpec((None, 1, D_MODEL), lambda i: (layer, 0, 0)),
                  row(512), row(256), row(256),
                  wspec(D_MODEL, N_BRANCH * D_MODEL), wspec(512, D_MODEL), wspec(256, D_MODEL),
                  wspec(256, D_MODEL), wspec(D_MODEL, D_MODEL)],
        out_specs=row(D_MODEL),
        out_shape=jax.ShapeDtypeStruct((n, D_MODEL), F32),
        compiler_params=pltpu.CompilerParams(dimension_semantics=("parallel",),
                                             vmem_limit_bytes=VMEM_LIMIT),
        name="merge",
    )(x2d, p["norm_attn"], oa, ob, oc, p["w_g"], p["w_out_a"], p["w_out_b"], p["w_out_c"], p["w_o"])


def _ffn_kernel(h_ref, g_ref, wg_ref, wu_ref, wd_ref, y_ref):
    h = h_ref[...]
    ms = jnp.mean(h * h, axis=-1, keepdims=True)
    hn = (h * lax.rsqrt(ms + EPS) * g_ref[...]).astype(BF16)
    gate = _dot(hn, wg_ref[...])
    up = _dot(hn, wu_ref[...])
    act = (gate * jax.nn.sigmoid(gate) * up).astype(BF16)
    y_ref[...] = h + _dot(act, wd_ref[...])


def _ffn_call(h2d, layer, p):
    n = h2d.shape[0]
    tm = min(TM_FFN, n)
    d_ff = p["w_gate"].shape[-1]
    row = lambda w: pl.BlockSpec((tm, w), lambda i: (i, 0))
    wspec = lambda r, c: pl.BlockSpec((None, r, c), lambda i: (layer, 0, 0), pipeline_mode=pl.Buffered(1))
    return pl.pallas_call(
        _ffn_kernel,
        grid=(n // tm,),
        in_specs=[row(D_MODEL), pl.BlockSpec((None, 1, D_MODEL), lambda i: (layer, 0, 0)),
                  wspec(D_MODEL, d_ff), wspec(D_MODEL, d_ff), wspec(d_ff, D_MODEL)],
        out_specs=row(D_MODEL),
        out_shape=jax.ShapeDtypeStruct((n, D_MODEL), F32),
        compiler_params=pltpu.CompilerParams(dimension_semantics=("parallel",),
                                             vmem_limit_bytes=VMEM_LIMIT),
        name="ffn",
    )(h2d, p["norm_ffn"], p["w_gate"], p["w_up"], p["w_down"])


QROWS = 16
N_CACHE = 8
DECODE_GROUP = 8


def _decode_kernel(pt_ref, lamp_ref, sub_ref,
                   qa_ref, qb_ref, qi_ref, qc_ref, iw_ref,
                   nak_ref, nav_ref, nbk_ref, nbv_ref, nik_ref, nck_ref, ncv_ref, nlf_ref,
                   *refs, lam_init, top_k, n_pages, group):
    cache_refs = [refs[c * group:(c + 1) * group] for c in range(N_CACHE)]
    ak_refs, av_refs, bk_refs, bv_refs, ki_refs, ck_refs, cv_refs, lft_refs = cache_refs
    (tri_ref, oa_ref, ob_ref, oc_ref,
     ma, la, acca, mc, lc, accc, carry, isc_s, sb_s, vb_s) = refs[N_CACHE * group:]
    j = pl.program_id(1)
    n_steps = n_pages // group
    psz = lft_refs[0].shape[1]
    qa = qa_ref[...]
    qb = qb_ref[...]
    qi = qi_ref[...]
    qc = qc_ref[...]
    iw = iw_ref[...]
    lane = lax.broadcasted_iota(jnp.int32, (1, psz), 1)

    def rowdot(q, krow):
        return jnp.sum(q.astype(F32) * krow.astype(F32), axis=-1, keepdims=True)

    @pl.when(j == 0)
    def _():
        ma[...] = rowdot(qa, nak_ref[...])
        la[...] = jnp.ones_like(la)
        acca[...] = nav_ref[...].astype(F32)
        mc[...] = rowdot(qc, nck_ref[...])
        lc[...] = jnp.ones_like(lc)
        accc[...] = jnp.broadcast_to(ncv_ref[...].astype(F32), accc.shape)
        carry[...] = jnp.broadcast_to(nlf_ref[...], carry.shape)
        own_i = jnp.sum(jnp.maximum(rowdot(qi, nik_ref[...]), 0.0) * iw, axis=0, keepdims=True)
        isc_s[...] = jnp.full_like(isc_s, NEG)
        isc_s[pl.ds(n_pages, 1), :] = jnp.where(lane == 0, own_i, NEG)
        sb_s[n_pages] = jnp.broadcast_to(rowdot(qb, nbk_ref[...]), (QROWS, psz))
        first = jnp.where(lane == 0, 1.0, 0.0).astype(F32)
        vb_s[n_pages] = (nbv_ref[...].astype(F32) * first).astype(BF16)

    def online(s, pv, m_ref, l_ref, acc_ref):
        m_prev = m_ref[...]
        m_new = jnp.maximum(m_prev, jnp.max(s, axis=-1, keepdims=True))
        alpha = jnp.exp(m_prev - m_new)
        p = jnp.exp(s - m_new)
        l_ref[...] = alpha * l_ref[...] + jnp.sum(p, axis=-1, keepdims=True)
        acc_ref[...] = alpha * acc_ref[...] + pv(p.astype(BF16))
        m_ref[...] = m_new

    n_rows = ak_refs[0].shape[0]
    key_head = lax.broadcasted_iota(jnp.int32, (QROWS, n_rows), 1) % H_A
    row_head = lax.broadcasted_iota(jnp.int32, (QROWS, n_rows), 0) // 2
    head_bias = jnp.where(key_head == row_head, 0.0, NEG)
    s_a = jnp.concatenate([_nt(qa, r[...].astype(BF16)) + head_bias for r in ak_refs], axis=1)

    def pv_a(p):
        return sum(_dot(p[:, g * n_rows:(g + 1) * n_rows], av_refs[g][...].astype(BF16)) for g in range(group))

    online(s_a, pv_a, ma, la, acca)
    run = carry[...][0:8]
    s_c = []
    for g in range(group):
        lft = jnp.concatenate([lft_refs[g][...], jnp.zeros((8 - H_C, psz), F32)], axis=0)
        suffix = jnp.dot(lft, tri_ref[...], preferred_element_type=F32, precision=lax.Precision.HIGHEST) + run
        bias_c = jnp.concatenate([suffix, jnp.zeros((QROWS - 8, psz), F32)], axis=0)
        s_c.append(_dot(qc, ck_refs[g][...].astype(BF16)) + bias_c)
        run = run + jnp.sum(lft, axis=-1, keepdims=True)
    carry[0:8] = run

    def pv_c(p):
        return sum(_nt(p[:, g * psz:(g + 1) * psz], cv_refs[g][...].astype(BF16)) for g in range(group))

    online(jnp.concatenate(s_c, axis=1), pv_c, mc, lc, accc)
    for g in range(group):
        page = n_pages - 1 - (j * group + g)
        si = jnp.maximum(_dot(qi, ki_refs[g][...].astype(BF16)), 0.0) * iw
        isc_s[pl.ds(page, 1), :] = jnp.sum(si, axis=0, keepdims=True)
        sb_s[page] = _dot(qb, bk_refs[g][...].astype(BF16))
        vb_s[page] = bv_refs[g][...].astype(BF16)

    @pl.when(j == n_steps - 1)
    def _():
        lp = lamp_ref[...]
        lam = (jnp.exp(jnp.sum(lp[0:1] * lp[1:2], axis=-1, keepdims=True))
               - jnp.exp(jnp.sum(lp[2:3] * lp[3:4], axis=-1, keepdims=True)) + lam_init)
        oa = acca[...] / la[...]
        for h in range(H_A):
            sl = slice(h * LANES, (h + 1) * LANES)
            o = oa[2 * h:2 * h + 1, :] - lam * oa[2 * h + 1:2 * h + 2, :]
            y = o * lax.rsqrt(jnp.mean(o * o, axis=-1, keepdims=True) + EPS) * sub_ref[...]
            oa_ref[:, sl] = (y * (1.0 - lam_init)).astype(oa_ref.dtype)
        oc = accc[...] / lc[...]
        for h in range(H_C):
            oc_ref[:, h * HEAD:(h + 1) * HEAD] = oc[h:h + 1, h * HEAD:(h + 1) * HEAD].astype(oc_ref.dtype)

        x = isc_s[...]
        valid = x > NEG * 0.5
        kt = float(top_k)
        idx = (lax.broadcasted_iota(jnp.int32, x.shape, 0) * psz
               + lax.broadcasted_iota(jnp.int32, x.shape, 1)).astype(F32)

        def count(pred):
            return jnp.sum(jnp.where(pred, 1.0, 0.0))

        vmax = jnp.max(x)
        vmin = jnp.min(jnp.where(valid, x, -NEG))
        c_max = count(x >= vmax)
        top_tied = c_max >= kt
        lo0 = jnp.where(top_tied, vmax, vmin)
        cnt0 = jnp.where(top_tied, c_max, count(valid))

        def active(lo, hi, cnt):
            mid = lo + (hi - lo) * 0.5
            return (cnt != kt) & (mid > lo) & (mid < hi)

        def bisect(st):
            lo, hi, cnt = st
            mid = lo + (hi - lo) * 0.5
            c = count(x >= mid)
            up = c >= kt
            return jnp.where(up, mid, lo), jnp.where(up, hi, mid), jnp.where(up, c, cnt)

        thr, _, cnt = lax.while_loop(lambda st: active(*st), bisect, (lo0, vmax, cnt0))
        need = kt - count(x > thr)
        n_keys = float(x.shape[0] * psz)

        def idx_step(_, st):
            lo_i, hi_i = st
            mid_i = jnp.floor((lo_i + hi_i) * 0.5)
            ok = count((x == thr) & (idx <= mid_i)) >= need
            return jnp.where(ok, lo_i, mid_i), jnp.where(ok, mid_i, hi_i)

        _, midx = lax.fori_loop(0, int(math.ceil(math.log2(n_keys))) + 1, idx_step,
                                (jnp.float32(-1.0), jnp.float32(n_keys - 1.0)))
        sel = (x > thr) | ((x == thr) & (idx <= midx))
        bias = jnp.where(sel & valid, 0.0, NEG)
        isc_s[...] = bias

        def pmax(p, m):
            return jnp.maximum(m, jnp.max(sb_s[p] + isc_s[pl.ds(p, 1), :], axis=-1, keepdims=True))

        mb = lax.fori_loop(0, n_pages + 1, pmax, jnp.full((QROWS, 1), NEG, F32), unroll=5)

        def pacc(p, st):
            l, acc = st
            pr = jnp.exp(sb_s[p] + isc_s[pl.ds(p, 1), :] - mb)
            return l + jnp.sum(pr, axis=-1, keepdims=True), acc + _nt(pr.astype(BF16), vb_s[p])

        lb, accb = lax.fori_loop(0, n_pages + 1, pacc,
                                 (jnp.zeros((QROWS, 1), F32), jnp.zeros((QROWS, H_B * D_B), F32)), unroll=5)
        ob = accb / lb
        for h in range(H_B):
            ob_ref[:, h * HEAD:(h + 1) * HEAD] = ob[h:h + 1, h * HEAD:(h + 1) * HEAD].astype(ob_ref.dtype)


def _decode_call(page_table, layer, lam_init, qmats, new_rows, caches, p):
    db, n_pages = page_table.shape
    lft = caches[-1]
    psz = lft.shape[3]
    top_k = min(TOPK_MAX, (n_pages * psz + 1) // 4)
    tri = jnp.triu(jnp.ones((psz, psz), F32), 1).T
    qa, qb, qi, qc, iw = qmats
    group = math.gcd(n_pages, DECODE_GROUP)

    def per_b(a):
        return pl.BlockSpec((None,) + a.shape[1:], lambda b, j, pt: (b,) + (0,) * (a.ndim - 1))

    def paged(a, g):
        return pl.BlockSpec((None, None) + a.shape[2:],
                            lambda b, j, pt: (layer, pt[b, n_pages - 1 - (j * group + g)], 0, 0))

    in_specs = ([pl.BlockSpec((None, 8, LANES), lambda b, j, pt: (layer, 0, 0)),
                 pl.BlockSpec((None, 1, LANES), lambda b, j, pt: (layer, 0, 0))]
                + [per_b(a) for a in (qa, qb, qi, qc, iw)]
                + [per_b(a) for a in new_rows]
                + [paged(a, g) for a in caches for g in range(group)]
                + [pl.BlockSpec((psz, psz), lambda b, j, pt: (0, 0))])
    out = lambda w: pl.BlockSpec((None, 1, w), lambda b, j, pt: (b, 0, 0))
    grid_spec = pltpu.PrefetchScalarGridSpec(
        num_scalar_prefetch=1, grid=(db, n_pages // group), in_specs=in_specs,
        out_specs=(out(512), out(256), out(256)),
        scratch_shapes=[pltpu.VMEM((QROWS, 1), F32), pltpu.VMEM((QROWS, 1), F32), pltpu.VMEM((QROWS, LANES), F32),
                        pltpu.VMEM((QROWS, 1), F32), pltpu.VMEM((QROWS, 1), F32), pltpu.VMEM((QROWS, 256), F32),
                        pltpu.VMEM((QROWS, 1), F32),
                        pltpu.VMEM((-(-(n_pages + 1) // 8) * 8, psz), F32),
                        pltpu.VMEM((n_pages + 1, QROWS, psz), F32),
                        pltpu.VMEM((n_pages + 1, H_B * D_B, psz), BF16)])
    return pl.pallas_call(
        functools.partial(_decode_kernel, lam_init=lam_init, top_k=top_k, n_pages=n_pages, group=group),
        grid_spec=grid_spec,
        out_shape=(jax.ShapeDtypeStruct((db, 1, 512), BF16), jax.ShapeDtypeStruct((db, 1, 256), BF16),
                   jax.ShapeDtypeStruct((db, 1, 256), BF16)),
        compiler_params=pltpu.CompilerParams(dimension_semantics=("parallel", "arbitrary"),
                                             vmem_limit_bytes=VMEM_LIMIT),
        name="decode",
    )(page_table, p["lamp"], p["subln_a"], qa, qb, qi, qc, iw, *new_rows,
      *[a for a in caches for _ in range(group)], tri)


def _rope_tables(pos):
    r = HEAD // ROT_FRAC
    half = r // 2
    inv = jnp.power(ROPE_THETA, -jnp.arange(half, dtype=F32) * 2.0 / r)
    ang = pos.astype(F32)[:, None] * inv[None, :]
    cos, sin = jnp.cos(ang), jnp.sin(ang)
    n = pos.shape[0]
    z = lambda w: jnp.zeros((n, w), F32)
    cos_p = jnp.concatenate([cos, cos, jnp.ones((n, HEAD - r), F32)], axis=1)
    sa_p = jnp.concatenate([-sin, z(HEAD - half)], axis=1)
    sb_p = jnp.concatenate([z(half), sin, z(HEAD - r)], axis=1)
    return tuple(jnp.tile(t, (1, LANES // HEAD)) for t in (cos_p, sa_p, sb_p))


def _prep_params(norm_attn, w_in, b_forget, q_norm_a, k_norm_a, lambda_q1, lambda_k1, lambda_q2, lambda_k2,
                 subln_a, q_norm_b, k_norm_b, k_norm_idx, q_norm_c, k_norm_c, w_out_a, w_out_b, w_out_c,
                 w_o, norm_ffn, w_gate, w_up, w_down):
    depth = w_in.shape[0]
    (aq, ak, av, bq, bk, bv, iq, ik, iw, cq, ck, cv, cf, gl) = jnp.split(
        w_in, np.cumsum(IN_WIDTHS)[:-1].tolist(), axis=-1)
    misc = jnp.concatenate([ik, iw, cf, jnp.zeros((depth, D_MODEL, LANES - D_IDX - H_IDX - H_C), F32)], axis=-1)
    w_mix = jnp.concatenate([aq, ak, av, bq, bk, bv, iq, misc, cq, ck, cv], axis=-1).astype(BF16)

    def row(v, reps, width=512, fill=1.0):
        t = jnp.tile(v, (1, reps))
        return jnp.concatenate([t, jnp.full((depth, width - t.shape[1]), fill, F32)], axis=1)

    misc_gain = jnp.concatenate([k_norm_idx, jnp.ones((depth, 512 - D_IDX), F32)], axis=1)
    bf_row = jnp.concatenate([jnp.zeros((depth, MISC_CF), F32), b_forget,
                              jnp.zeros((depth, 512 - MISC_CF - H_C), F32)], axis=1)
    pvec = jnp.stack([row(q_norm_a, 8), row(k_norm_a, 8), row(q_norm_b, 4), row(k_norm_b, 4),
                      row(q_norm_c, 4), row(k_norm_c, 4), misc_gain, bf_row], axis=1)
    zpad = jnp.zeros((depth, LANES - DK_A), F32)
    lam_rows = [jnp.concatenate([v, zpad], axis=1) for v in (lambda_q1, lambda_k1, lambda_q2, lambda_k2)]
    lamp = jnp.stack(lam_rows + [jnp.zeros((depth, LANES), F32)] * 4, axis=1)
    blk = np.kron(np.eye(512 // HEAD), np.ones((HEAD, HEAD))) / HEAD
    return dict(
        norm_attn=norm_attn[:, None, :], w_mix=w_mix, w_g=gl.astype(BF16), pvec=pvec,
        bd=jnp.asarray(blk, BF16), lamp=lamp, subln_a=subln_a[:, None, :], subln_col=subln_a[:, :, None],
        w_out_a=w_out_a.astype(BF16), w_out_b=w_out_b.astype(BF16), w_out_c=w_out_c.astype(BF16),
        w_o=w_o.astype(BF16), norm_ffn=norm_ffn[:, None, :],
        w_gate=w_gate.astype(BF16), w_up=w_up.astype(BF16), w_down=w_down.astype(BF16))


def _pair_rows(even, odd):
    db, h, w = even.shape
    rows = jnp.stack([even, odd], axis=2).reshape(db, 2 * h, w)
    return jnp.concatenate([rows, jnp.zeros((db, QROWS - 2 * h, w), rows.dtype)], axis=1)


def _decode_qmats(aq1z, aq2z, bqz, iqz, cqz, iw):
    db = aq1z.shape[0]
    qa = _pair_rows(aq1z.reshape(db, H_A, LANES), aq2z.reshape(db, H_A, LANES))

    def unpad(qz, n_heads):
        return qz.reshape(db, n_heads // 2, 2, LANES).sum(axis=2).reshape(db, n_heads * HEAD)

    def head_rows(q, n_heads):
        m = np.zeros((QROWS, n_heads * HEAD), np.float32)
        for h in range(n_heads):
            m[h, h * HEAD:(h + 1) * HEAD] = 1.0
        return q[:, None, :] * jnp.asarray(m, BF16)

    qb = head_rows(unpad(bqz, H_B), H_B)
    qc = head_rows(unpad(cqz, H_C), H_C)
    iq = unpad(iqz, H_IDX).reshape(db, H_IDX, HEAD)
    qi = jnp.concatenate([iq, jnp.zeros((db, QROWS - H_IDX, HEAD), BF16)], axis=1)
    iwc = jnp.concatenate([iw, jnp.zeros((db, QROWS - H_IDX), F32)], axis=1)[:, :, None]
    return qa, qb, qi, qc, iwc


def kernel(x_prompt, x_sample, cache_a_k, cache_a_v, cache_b_k, cache_b_v, cache_b_kidx, cache_c_k, cache_c_v, cache_c_logf, page_table, norm_attn, w_in, b_forget, q_norm_a, k_norm_a, lambda_q1, lambda_k1, lambda_q2, lambda_k2, subln_a, q_norm_b, k_norm_b, k_norm_idx, q_norm_c, k_norm_c, w_out_a, w_out_b, w_out_c, w_o, norm_ffn, w_gate, w_up, w_down):
    bsz, t, _ = x_prompt.shape
    db, ds, _ = x_sample.shape
    assert ds == 1
    depth = w_in.shape[0]
    n_pool, psz = cache_a_k.shape[1], cache_a_k.shape[2]
    past_len = page_table.shape[1] * psz
    p = _prep_params(norm_attn, w_in, b_forget, q_norm_a, k_norm_a, lambda_q1, lambda_k1, lambda_q2,
                     lambda_k2, subln_a, q_norm_b, k_norm_b, k_norm_idx, q_norm_c, k_norm_c,
                     w_out_a, w_out_b, w_out_c, w_o, norm_ffn, w_gate, w_up, w_down)
    tabs_p = _rope_tables(jnp.arange(t, dtype=jnp.int32))
    tabs_s = _rope_tables(jnp.full((db,), past_len, jnp.int32))
    rows_view = lambda c: c.reshape(depth, n_pool, psz * c.shape[3], c.shape[4])
    tr_view = lambda c: jnp.transpose(c, (0, 1, 3, 4, 2)).reshape(depth, n_pool, c.shape[3] * c.shape[4], psz)
    caches = (rows_view(cache_a_k), rows_view(cache_a_v), tr_view(cache_b_k), tr_view(cache_b_v),
              jnp.swapaxes(cache_b_kidx, 2, 3), tr_view(cache_c_k), tr_view(cache_c_v),
              jnp.swapaxes(cache_c_logf, 2, 3))

    xp = x_prompt.reshape(bsz * t, D_MODEL)
    xs = x_sample.reshape(db, D_MODEL)
    rows_p, rows_s = [], []
    for l in range(depth):
        lam_init = 0.8 - 0.6 * math.exp(-0.3 * l)
        (ak, av, bk, bv, ik, ck, cv, lf, aq1, aq2, akb, avb, bqz, bkb, bvb, iqz, ik2, iw, cqz, ckb, cvb,
         cc) = _proj_call(xp, l, p, tabs_p, t, True)
        rows_p.append((ak, av, bk, bv, ik, ck, cv, lf))
        oa = _attn_a_call(aq1, aq2, akb, avb, p["lamp"], p["subln_col"], l, bsz, t, lam_init)
        iwt = jnp.swapaxes(iw.reshape(bsz, t, H_IDX), 1, 2)
        ob = _mixer_b_call(iqz, iwt, bqz, ik2, bkb, bvb, bsz, t)
        cct = jnp.swapaxes(cc.reshape(bsz, t, H_C), 1, 2)
        cct = jnp.concatenate([cct, jnp.zeros((bsz, 8 - H_C, t), F32)], axis=1)
        oc = _attn_c_call(cqz, ckb, cvb, cc, cct, bsz, t)
        xp = _ffn_call(_merge_call(xp, oa, ob, oc, l, p), l, p)
        (ak, av, bk, bv, ik, ck, cv, lf, aq1, aq2, akb, avb, bqz, bkb, bvb, iqz, ik2, iw, cqz, ckb, cvb,
         _) = _proj_call(xs, l, p, tabs_s, db, False)
        rows_s.append((ak, av, bk, bv, ik, ck, cv, lf))
        qmats = _decode_qmats(aq1, aq2, bqz, iqz, cqz, iw)
        lf8 = jnp.concatenate([lf, jnp.zeros((db, QROWS - H_C), F32)], axis=1)[:, :, None]
        ak_h = akb.reshape(db, H_A, LANES)
        av_h = avb.reshape(db, H_A, LANES)
        new_rows = (_pair_rows(ak_h, ak_h), _pair_rows(av_h, av_h), bkb[:, None, :],
                    bvb.astype(F32)[:, :, None], ik2[:, None, 0:D_IDX], ckb[:, None, :], cvb[:, None, :], lf8)
        oa, ob, oc = _decode_call(page_table, l, lam_init, qmats, new_rows, caches, p)
        xs = _ffn_call(_merge_call(xs, oa.reshape(db, 512), ob.reshape(db, 256), oc.reshape(db, 256), l, p),
                       l, p)

    def stack(rows, idx, lead, tail):
        return jnp.stack([r[idx] for r in rows], axis=0).reshape((depth,) + lead + tail)

    tails = ((H_A, 2 * DK_A), (H_A, DV_A), (H_B, D_B), (H_B, D_B), (D_IDX,), (H_C, D_C), (H_C, D_C), (H_C,))
    outs_p = [stack(rows_p, i, (bsz, t), tails[i]) for i in range(8)]
    outs_s = [stack(rows_s, i, (db, 1), tails[i]) for i in range(8)]
    return (xp.reshape(bsz, t, D_MODEL), xs.reshape(db, 1, D_MODEL), *outs_p, *outs_s)
```
